```python
import math
import jax, jax.numpy as jnp
from jax import lax
import numpy as np

D_MODEL = 1024
BATCH = 4
SEQ = 8192
DEPTH = 4
DEC_BATCH = 32
DEC_SEQ = 2048
PAST_LEN = 128

HEAD_DIM = 64
N_HEADS = 8
N_KV_HEADS = 2
GQ = N_HEADS // N_KV_HEADS
Q_DIM = N_HEADS * HEAD_DIM
KV_DIM = N_KV_HEADS * HEAD_DIM
WINDOW = 128
BLK = 128
N_BUCKETS = 32
MAX_DISTANCE = 128
D_INNER = 1024
SSD_HEAD_DIM = 64
N_SSD_HEADS = D_INNER // SSD_HEAD_DIM
N_SSD_GROUPS = 2
HPG = N_SSD_HEADS // N_SSD_GROUPS
D_STATE = 64
CHUNK = 128
CONV_WIDTH = 5
CONV_PAD = CONV_WIDTH // 2
BC_DIM = N_SSD_GROUPS * D_STATE
CONV_DIM = D_INNER + 2 * BC_DIM
N_EGROUPS = 4
EXPERTS_PER_GROUP = 4
N_EXPERTS = N_EGROUPS * EXPERTS_PER_GROUP
TOP_K_INNER = 2
D_EXPERT = 256
PLE_DIM = 256
EPS = 1e-6

SPLIT_SIZES = [Q_DIM, KV_DIM, KV_DIM, D_INNER, CONV_DIM, N_SSD_HEADS, N_SSD_HEADS, D_MODEL, D_MODEL]
IN_COLS = int(sum(SPLIT_SIZES))
SPLITS = [int(s) for s in np.cumsum(SPLIT_SIZES)[:-1]]

kernel_name = "hybrid_swa_ssd_hmoe_encoder"


def rmsnorm(x, g):
    xf = x.astype(jnp.float32)
    y = xf * lax.rsqrt(jnp.mean(xf * xf, axis=-1, keepdims=True) + EPS)
    return (y * g.astype(jnp.float32)).astype(x.dtype)


def _t5_bucket_band():
    rel = (np.arange(3 * BLK)[None, :] - BLK) - np.arange(BLK)[:, None]
    nb = N_BUCKETS // 2
    max_exact = nb // 2
    ret = (rel > 0).astype(np.int32) * nb
    n = np.abs(rel)
    large = max_exact + (np.log(np.maximum(n, 1) / max_exact) / np.log(MAX_DISTANCE / max_exact)
                         * (nb - max_exact)).astype(np.int32)
    large = np.minimum(large, nb - 1)
    return (ret + np.where(n < max_exact, n, large)).astype(np.int32), rel


def band_bias(rel_bias):
    bucket, _ = _t5_bucket_band()
    b = rel_bias[jnp.asarray(bucket)]
    b = jnp.transpose(b, (2, 0, 1)).astype(jnp.float32)
    return b.reshape(N_KV_HEADS, GQ, BLK, 3 * BLK)


def windowed_gqa(q, k, v, sink, bias):
    b, l = q.shape[:2]
    nb = l // BLK
    qb = q.reshape(b, nb, BLK, N_KV_HEADS, GQ, HEAD_DIM).astype(jnp.float32)
    pad = ((0, 0), (BLK, BLK), (0, 0), (0, 0))
    kp = jnp.pad(k, pad).reshape(b, nb + 2, BLK, N_KV_HEADS, HEAD_DIM)
    vp = jnp.pad(v, pad).reshape(b, nb + 2, BLK, N_KV_HEADS, HEAD_DIM)
    kb = jnp.concatenate([kp[:, :-2], kp[:, 1:-1], kp[:, 2:]], axis=2).astype(jnp.float32)
    vb = jnp.concatenate([vp[:, :-2], vp[:, 1:-1], vp[:, 2:]], axis=2).astype(jnp.float32)
    _, rel = _t5_bucket_band()
    kpos = np.arange(nb)[:, None, None] * BLK - BLK + np.arange(3 * BLK)[None, None, :]
    valid = (np.abs(rel)[None] <= WINDOW) & (kpos >= 0) & (kpos < l)
    s = jnp.einsum('bnqhgd,bnjhd->bnhgqj', qb, kb) * (HEAD_DIM ** -0.5) + bias
    s = jnp.where(jnp.asarray(valid)[None, :, None, None], s, -1e30)
    sink6 = sink.astype(jnp.float32)[None, None, :, :, None, None]
    m = jnp.maximum(jnp.max(s, axis=-1, keepdims=True), sink6)
    pe = jnp.exp(s - m)
    den = jnp.sum(pe, axis=-1, keepdims=True) + jnp.exp(sink6 - m)
    o = jnp.einsum('bnhgqj,bnjhd->bnqhgd', pe / den, vb)
    return o.reshape(b, l, Q_DIM).astype(q.dtype)


def depthwise_conv(u, w, bconv):
    out = lax.conv_general_dilated(u, w[:, None, :], (1,), [(CONV_PAD, CONV_PAD)],
                                   dimension_numbers=('NWC', 'WIO', 'NWC'),
                                   feature_group_count=CONV_DIM)
    return out + bconv


def ssd_chunked(x, dt, A, B, C):
    b, l = x.shape[:2]
    nc = l // CHUNK
    x = x.astype(jnp.float32).reshape(b, nc, CHUNK, N_SSD_GROUPS, HPG, SSD_HEAD_DIM)
    dt = dt.reshape(b, nc, CHUNK, N_SSD_GROUPS, HPG)
    B = B.astype(jnp.float32).reshape(b, nc, CHUNK, N_SSD_GROUPS, D_STATE)
    C = C.astype(jnp.float32).reshape(b, nc, CHUNK, N_SSD_GROUPS, D_STATE)
    acs = jnp.cumsum(dt * A, axis=2)
    xdt = x * dt[..., None]
    tril = jnp.tril(jnp.ones((CHUNK, CHUNK), dtype=bool))
    seg = acs[:, :, :, None] - acs[:, :, None]
    Lmat = jnp.exp(jnp.where(tril[:, :, None, None], seg, -jnp.inf))
    cb = jnp.einsum('bcign,bcjgn->bcijg', C, B)
    y_diag = jnp.einsum('bcijg,bcijgh,bcjghp->bcighp', cb, Lmat, xdt)
    decay_states = jnp.exp(acs[:, :, -1:] - acs)
    states = jnp.einsum('bcjgn,bcjghp->bcghpn', B, xdt * decay_states[..., None])
    chunk_decay = jnp.exp(acs[:, :, -1])

    def step(h, inp):
        s, d = inp
        return h * d[..., None, None] + s, h

    h0 = jnp.zeros((b, N_SSD_GROUPS, HPG, SSD_HEAD_DIM, D_STATE), jnp.float32)
    _, h_prev = lax.scan(step, h0, (jnp.moveaxis(states, 1, 0), jnp.moveaxis(chunk_decay, 1, 0)))
    h_prev = jnp.moveaxis(h_prev, 0, 1)
    y_off = jnp.einsum('bcign,bcghpn->bcighp', C, h_prev) * jnp.exp(acs)[..., None]
    return (y_diag + y_off).reshape(b, l, N_SSD_GROUPS, HPG, SSD_HEAD_DIM)


def ssd_branch(z, xbc, dtf, dtb, conv_w, conv_b, a_log_f, a_log_b, dt_bias_f, dt_bias_b, d_skip, norm_g):
    b, l, _ = z.shape
    xbc = jax.nn.silu(depthwise_conv(xbc, conv_w, conv_b))
    xs, Bm, Cm = jnp.split(xbc, [D_INNER, D_INNER + BC_DIM], axis=-1)
    xs = xs.reshape(b, l, N_SSD_GROUPS, HPG, SSD_HEAD_DIM)
    Bm = Bm.reshape(b, l, N_SSD_GROUPS, D_STATE)
    Cm = Cm.reshape(b, l, N_SSD_GROUPS, D_STATE)
    dt_f = jax.nn.softplus((dtf + dt_bias_f).astype(jnp.float32)).reshape(b, l, N_SSD_GROUPS, HPG)
    dt_b = jax.nn.softplus((dtb + dt_bias_b).astype(jnp.float32)).reshape(b, l, N_SSD_GROUPS, HPG)
    A_f = -jnp.exp(a_log_f.astype(jnp.float32)).reshape(N_SSD_GROUPS, HPG)
    A_b = -jnp.exp(a_log_b.astype(jnp.float32)).reshape(N_SSD_GROUPS, HPG)
    fl = lambda t: jnp.flip(t, axis=1)
    y = ssd_chunked(xs, dt_f, A_f, Bm, Cm) + fl(ssd_chunked(fl(xs), fl(dt_b), A_b, fl(Bm), fl(Cm)))
    y = y + d_skip.astype(jnp.float32).reshape(N_SSD_GROUPS, HPG)[:, :, None] * xs.astype(jnp.float32)
    gsz = D_INNER // N_SSD_GROUPS
    y = y.reshape(b, l, N_SSD_GROUPS, gsz) * jax.nn.silu(z.astype(jnp.float32)).reshape(b, l, N_SSD_GROUPS, gsz)
    y = y * lax.rsqrt(jnp.mean(y * y, axis=-1, keepdims=True) + EPS)
    y = y * norm_g.astype(jnp.float32).reshape(N_SSD_GROUPS, gsz)
    return y.reshape(b, l, D_INNER).astype(z.dtype)


def hier_moe(h, router_g, router_g_b, router_e, router_e_b, w_gate_up, w_down):
    b, l, _ = h.shape
    lg = (h @ router_g + router_g_b).astype(jnp.float32)
    pg = jax.nn.softmax(lg, axis=-1)
    p_sel, g_sel = lax.top_k(pg, 1)
    le = (h @ router_e + router_e_b).astype(jnp.float32).reshape(b, l, N_EGROUPS, EXPERTS_PER_GROUP)
    le_sel = jnp.take_along_axis(le, g_sel[..., None], axis=2)[:, :, 0]
    pe = jax.nn.softmax(le_sel, axis=-1)
    vals, idx = lax.top_k(pe, TOP_K_INNER)
    vals = vals / jnp.sum(vals, axis=-1, keepdims=True)
    w_inner = jnp.sum(jax.nn.one_hot(idx, EXPERTS_PER_GROUP, dtype=jnp.float32) * vals[..., None], axis=-2)
    gates = (jax.nn.one_hot(g_sel[..., 0], N_EGROUPS, dtype=jnp.float32)[..., None]
             * w_inner[:, :, None, :] * p_sel[..., None]).reshape(b, l, N_EXPERTS).astype(h.dtype)

    def per_seq(args):
        hs, gs = args
        gu = jnp.einsum('ld,edf->lef', hs, w_gate_up)
        g, u = jnp.split(gu, 2, axis=-1)
        a = jax.nn.silu(g) * u * gs[..., None]
        return jnp.einsum('lef,efd->ld', a, w_down)

    return lax.map(per_seq, (h, gates))


def trunk(x, p, P):
    b, l, _ = x.shape
    bias = band_bias(P['rel_bias'])
    for i in range(DEPTH):
        h = rmsnorm(x, P['g_mix'][i])
        proj = h @ P['w_in'][i]
        q, k, v, z, xbc, dtf, dtb, ga, gs = jnp.split(proj, SPLITS, axis=-1)
        att = windowed_gqa(q.reshape(b, l, N_KV_HEADS, GQ, HEAD_DIM),
                           k.reshape(b, l, N_KV_HEADS, HEAD_DIM),
                           v.reshape(b, l, N_KV_HEADS, HEAD_DIM),
                           P['attn_sink'][i].reshape(N_KV_HEADS, GQ), bias)
        ya = att @ P['w_attn_out'][i]
        ys = ssd_branch(z, xbc, dtf, dtb, P['conv_w'][i], P['conv_b'][i], P['a_log_f'][i], P['a_log_b'][i],
                        P['dt_bias_f'][i], P['dt_bias_b'][i], P['d_skip'][i], P['ssd_norm_g'][i]) @ P['w_ssd_out'][i]
        mixed = jax.nn.sigmoid(ga) * ya + jax.nn.sigmoid(gs) * ys
        x = x + mixed @ P['w_out'][i]
        x = x + hier_moe(rmsnorm(x, P['g_ffn'][i]), P['router_g'][i], P['router_g_b'][i],
                         P['router_e'][i], P['router_e_b'][i], P['w_gate_up'][i], P['w_down'][i])
        x = x + jax.nn.sigmoid(rmsnorm(x, P['g_ple'][i]) @ P['w_ple_gate'][i]) * (p[i] @ P['w_ple_proj'][i])
    return rmsnorm(x, P['g_final'])


def setup_inputs(seed: int = 0) -> dict:
    key = jax.random.key(seed)
    ks = iter(jax.random.split(key, 40))
    nrm = lambda shape, scale: jax.random.normal(next(ks), shape, jnp.float32) * scale

    def dt_bias(shape):
        dt = jnp.exp(jax.random.uniform(next(ks), shape, jnp.float32, math.log(1e-3), math.log(1e-1)))
        return dt + jnp.log(-jnp.expm1(-dt))

    def a_log(shape):
        return jnp.log(jax.random.uniform(next(ks), shape, jnp.float32, 1.0, 16.0))

    return {
        "x_prompt": nrm((BATCH, SEQ, D_MODEL), 1.0),
        "x_sample": nrm((DEC_BATCH, DEC_SEQ, D_MODEL), 1.0),
        "p_prompt": nrm((DEPTH, BATCH, SEQ, PLE_DIM), 1.0),
        "p_sample": nrm((DEPTH, DEC_BATCH, DEC_SEQ, PLE_DIM), 1.0),
        "g_mix": 1.0 + nrm((DEPTH, D_MODEL), 0.02),
        "w_in": nrm((DEPTH, D_MODEL, IN_COLS), D_MODEL ** -0.5),
        "rel_bias": nrm((N_BUCKETS, N_HEADS), 0.5),
        "attn_sink": nrm((DEPTH, N_HEADS), 0.5),
        "w_attn_out": nrm((DEPTH, Q_DIM, D_MODEL), Q_DIM ** -0.5),
        "conv_w": nrm((DEPTH, CONV_WIDTH, CONV_DIM), CONV_WIDTH ** -0.5),
        "conv_b": nrm((DEPTH, CONV_DIM), 0.02),
        "a_log_f": a_log((DEPTH, N_SSD_HEADS)),
        "a_log_b": a_log((DEPTH, N_SSD_HEADS)),
        "dt_bias_f": dt_bias((DEPTH, N_SSD_HEADS)),
        "dt_bias_b": dt_bias((DEPTH, N_SSD_HEADS)),
        "d_skip": 1.0 + nrm((DEPTH, N_SSD_HEADS), 0.02),
        "ssd_norm_g": 1.0 + nrm((DEPTH, D_INNER), 0.02),
        "w_ssd_out": nrm((DEPTH, D_INNER, D_MODEL), D_INNER ** -0.5),
        "w_out": nrm((DEPTH, D_MODEL, D_MODEL), D_MODEL ** -0.5),
        "g_ffn": 1.0 + nrm((DEPTH, D_MODEL), 0.02),
        "router_g": nrm((DEPTH, D_MODEL, N_EGROUPS), D_MODEL ** -0.5),
        "router_g_b": nrm((DEPTH, N_EGROUPS), 0.01),
        "router_e": nrm((DEPTH, D_MODEL, N_EXPERTS), D_MODEL ** -0.5),
        "router_e_b": nrm((DEPTH, N_EXPERTS), 0.01),
        "w_gate_up": nrm((DEPTH, N_EXPERTS, D_MODEL, 2 * D_EXPERT), D_MODEL ** -0.5),
        "w_down": nrm((DEPTH, N_EXPERTS, D_EXPERT, D_MODEL), D_EXPERT ** -0.5),
        "g_ple": 1.0 + nrm((DEPTH, D_MODEL), 0.02),
        "w_ple_gate": nrm((DEPTH, D_MODEL, D_MODEL), D_MODEL ** -0.5),
        "w_ple_proj": nrm((DEPTH, PLE_DIM, D_MODEL), PLE_DIM ** -0.5),
        "g_final": 1.0 + nrm((D_MODEL,), 0.02),
    }


def reference(x_prompt, x_sample, p_prompt, p_sample, g_mix, w_in, rel_bias, attn_sink, w_attn_out,
              conv_w, conv_b, a_log_f, a_log_b, dt_bias_f, dt_bias_b, d_skip, ssd_norm_g, w_ssd_out,
              w_out, g_ffn, router_g, router_g_b, router_e, router_e_b, w_gate_up, w_down,
              g_ple, w_ple_gate, w_ple_proj, g_final):
    P = dict(g_mix=g_mix, w_in=w_in, rel_bias=rel_bias, attn_sink=attn_sink, w_attn_out=w_attn_out,
             conv_w=conv_w, conv_b=conv_b, a_log_f=a_log_f, a_log_b=a_log_b, dt_bias_f=dt_bias_f,
             dt_bias_b=dt_bias_b, d_skip=d_skip, ssd_norm_g=ssd_norm_g, w_ssd_out=w_ssd_out,
             w_out=w_out, g_ffn=g_ffn, router_g=router_g, router_g_b=router_g_b, router_e=router_e,
             router_e_b=router_e_b, w_gate_up=w_gate_up, w_down=w_down, g_ple=g_ple,
             w_ple_gate=w_ple_gate, w_ple_proj=w_ple_proj, g_final=g_final)
    y_prompt = trunk(x_prompt, p_prompt, P)
    y_sample = trunk(x_sample, p_sample, P)
    return (y_prompt, y_sample)
```

```python
import functools

import numpy as np
import jax
import jax.numpy as jnp
from jax import lax
from jax.experimental import pallas as pl
from jax.experimental.pallas import tpu as pltpu

F32 = jnp.float32
BF16 = jnp.bfloat16

LANE = 128
HALF_LANE = LANE // 2
VMEM_LIMIT_BYTES = 56 * 1024 * 1024

D_MODEL = 1024
HEAD_DIM = 64
N_HEADS = 8
N_KV_HEADS = 2
GQ = N_HEADS // N_KV_HEADS
Q_DIM = N_HEADS * HEAD_DIM
KV_DIM = N_KV_HEADS * HEAD_DIM
WINDOW = 128
BLK = 128
N_BUCKETS = 32
MAX_DISTANCE = 128
D_INNER = 1024
SSD_HEAD_DIM = 64
N_SSD_HEADS = D_INNER // SSD_HEAD_DIM
N_SSD_GROUPS = 2
D_STATE = 64
CHUNK = 128
CONV_WIDTH = 5
CONV_PAD = CONV_WIDTH // 2
BC_DIM = N_SSD_GROUPS * D_STATE
CONV_DIM = D_INNER + 2 * BC_DIM
N_EGROUPS = 4
EXPERTS_PER_GROUP = 4
N_EXPERTS = N_EGROUPS * EXPERTS_PER_GROUP
D_EXPERT = 256
PLE_DIM = 256
EPS = 1e-6
MASKED = -1e30

QKV_COLS = Q_DIM + 2 * 2 * KV_DIM
GATE_COLS = 2 * D_MODEL
DT_COLS = LANE
IN_COLS_PADDED = QKV_COLS + D_INNER + CONV_DIM + GATE_COLS + DT_COLS
HALO = 16


def _cparams(*sem):
    return pltpu.CompilerParams(dimension_semantics=sem, vmem_limit_bytes=VMEM_LIMIT_BYTES)


def _dot(a, b):
    return jnp.dot(a, b, preferred_element_type=F32)


def _dot_nt(a, b):
    return lax.dot_general(a, b, (((1,), (1,)), ((), ())), preferred_element_type=F32)


def _split_bf16(v, terms):
    parts = []
    r = v
    for _ in range(terms):
        p = r.astype(BF16)
        parts.append(p)
        r = r - p.astype(F32)
    return parts


def _rmsnorm(x, g):
    return x * lax.rsqrt(jnp.mean(x * x, axis=-1, keepdims=True) + EPS) * g


def _sigmoid(x):
    return 1.0 / (1.0 + jnp.exp(-x))


def _inproj_kernel(x_ref, g_ref, w_ref, qkv_ref, z_ref, xbc_ref, gag_ref, dt_ref):
    hb = _rmsnorm(x_ref[...], g_ref[...]).astype(BF16)
    off = 0
    for ref in (qkv_ref, z_ref, xbc_ref, gag_ref, dt_ref):
        n = ref.shape[-1]
        ref[...] = _dot(hb, w_ref[:, off:off + n]).astype(ref.dtype)
        off += n


def _inproj(x, g, w, tm):
    t = x.shape[0]
    widths = (QKV_COLS, D_INNER, CONV_DIM, GATE_COLS, DT_COLS)
    dtypes = (BF16, BF16, BF16, BF16, F32)
    return pl.pallas_call(
        _inproj_kernel,
        grid=(t // tm,),
        in_specs=[
            pl.BlockSpec((tm, D_MODEL), lambda i: (i, 0)),
            pl.BlockSpec((1, D_MODEL), lambda i: (0, 0)),
            pl.BlockSpec((D_MODEL, IN_COLS_PADDED), lambda i: (0, 0)),
        ],
        out_specs=[pl.BlockSpec((tm, n), lambda i: (i, 0)) for n in widths],
        out_shape=[jax.ShapeDtypeStruct((t, n), d) for n, d in zip(widths, dtypes)],
        compiler_params=_cparams("parallel"),
        name="inproj",
    )(x, g, w)


def _attn_kernel(sink_ref, q_ref, kp_ref, km_ref, kn_ref, vp_ref, vm_ref, vn_ref, bias_ref, o_ref,
                 k_ext, v_ext, *, qb, seq):
    n = pl.program_id(1)
    k_ext[0:BLK] = kp_ref[0]
    k_ext[BLK:BLK + qb] = km_ref[0]
    k_ext[BLK + qb:] = kn_ref[0]
    v_ext[0:BLK] = vp_ref[0]
    v_ext[BLK:BLK + qb] = vm_ref[0]
    v_ext[BLK + qb:] = vn_ref[0]
    low = lax.broadcasted_iota(jnp.int32, (BLK, LANE), 1) < HALF_LANE
    kcol = lax.broadcasted_iota(jnp.int32, (1, 3 * BLK), 1)

    def sub_block(j, carry):
        r0 = pl.multiple_of(j * BLK, BLK)
        kpos = n * qb + r0 - BLK + kcol
        valid = (kpos >= 0) & (kpos < seq)
        k3 = k_ext[pl.ds(r0, 3 * BLK), :]
        v3 = v_ext[pl.ds(r0, 3 * BLK), :]
        for pair in range(N_HEADS // 2):
            kv = pair // (GQ // 2)
            qp = q_ref[0, pl.ds(r0, BLK), pair * LANE:(pair + 1) * LANE]
            kk = k3[:, kv * LANE:(kv + 1) * LANE]
            vv = v3[:, kv * LANE:(kv + 1) * LANE]
            outs = []
            for half in range(2):
                hq = 2 * pair + half
                qm = jnp.where(low if half == 0 else jnp.logical_not(low), qp, jnp.zeros_like(qp))
                s = _dot_nt(qm, kk) + bias_ref[hq]
                s = jnp.where(valid, s, MASKED)
                sk = sink_ref[hq]
                m = jnp.maximum(jnp.max(s, axis=-1, keepdims=True), sk)
                p = jnp.exp(s - m)
                den = jnp.sum(p, axis=-1, keepdims=True) + jnp.exp(sk - m)
                outs.append(_dot(p.astype(BF16), vv) / den)
            o_ref[0, pl.ds(r0, BLK), pair * LANE:(pair + 1) * LANE] = (
                jnp.where(low, outs[0], outs[1]).astype(o_ref.dtype))
        return carry

    lax.fori_loop(0, qb // BLK, sub_block, 0)


def _attention(qkv, sink, bias, qb):
    b, l, _ = qkv.shape
    nq = l // qb
    r = qb // BLK
    nblk = l // BLK
    kcol, vcol = Q_DIM // (2 * LANE), Q_DIM // (2 * LANE) + 1

    def prev_map(c):
        return lambda bi, n: (bi, jnp.maximum(n * r - 1, 0), c)

    def next_map(c):
        return lambda bi, n: (bi, jnp.minimum((n + 1) * r, nblk - 1), c)

    kvw = 2 * LANE
    return pl.pallas_call(
        functools.partial(_attn_kernel, qb=qb, seq=l),
        grid=(b, nq),
        in_specs=[
            pl.BlockSpec(memory_space=pltpu.SMEM),
            pl.BlockSpec((1, qb, Q_DIM), lambda bi, n: (bi, n, 0)),
            pl.BlockSpec((1, BLK, kvw), prev_map(kcol)),
            pl.BlockSpec((1, qb, kvw), lambda bi, n: (bi, n, kcol)),
            pl.BlockSpec((1, BLK, kvw), next_map(kcol)),
            pl.BlockSpec((1, BLK, kvw), prev_map(vcol)),
            pl.BlockSpec((1, qb, kvw), lambda bi, n: (bi, n, vcol)),
            pl.BlockSpec((1, BLK, kvw), next_map(vcol)),
            pl.BlockSpec((N_HEADS, BLK, 3 * BLK), lambda bi, n: (0, 0, 0)),
        ],
        out_specs=pl.BlockSpec((1, qb, Q_DIM), lambda bi, n: (bi, n, 0)),
        out_shape=jax.ShapeDtypeStruct((b, l, Q_DIM), BF16),
        scratch_shapes=[pltpu.VMEM((qb + 2 * BLK, kvw), BF16), pltpu.VMEM((qb + 2 * BLK, kvw), BF16)],
        compiler_params=_cparams("parallel", "parallel"),
        name="attention",
    )(sink, qkv, qkv, qkv, qkv, qkv, qkv, qkv, bias)


def _conv_kernel(p_ref, m_ref, n_ref, w_ref, b_ref, xs_ref, bc_ref, ext, *, tm, nblk):
    i = pl.program_id(1)
    ext[0:HALO] = jnp.where(i > 0, p_ref[0].astype(F32), 0.0)
    ext[HALO:HALO + tm] = m_ref[0].astype(F32)
    ext[HALO + tm:] = jnp.where(i < nblk - 1, n_ref[0].astype(F32), 0.0)
    for cb in range(CONV_DIM // LANE):
        cs = slice(cb * LANE, (cb + 1) * LANE)
        acc = b_ref[:, cs] + w_ref[0:1, cs] * ext[HALO - CONV_PAD:HALO - CONV_PAD + tm, cs]
        for k in range(1, CONV_WIDTH):
            acc = acc + w_ref[k:k + 1, cs] * ext[HALO - CONV_PAD + k:HALO - CONV_PAD + k + tm, cs]
        y = (acc * _sigmoid(acc))
        if cb < D_INNER // LANE:
            xs_ref[0, :, cs] = y.astype(xs_ref.dtype)
        else:
            c0 = cb * LANE - D_INNER
            bc_ref[0, :, c0:c0 + LANE] = y.astype(bc_ref.dtype)


def _conv(xbc, w, bvec, tm):
    b, l, _ = xbc.shape
    nblk = l // tm
    r = tm // HALO
    nh = l // HALO
    return pl.pallas_call(
        functools.partial(_conv_kernel, tm=tm, nblk=nblk),
        grid=(b, nblk),
        in_specs=[
            pl.BlockSpec((1, HALO, CONV_DIM), lambda bi, i: (bi, jnp.maximum(i * r - 1, 0), 0)),
            pl.BlockSpec((1, tm, CONV_DIM), lambda bi, i: (bi, i, 0)),
            pl.BlockSpec((1, HALO, CONV_DIM), lambda bi, i: (bi, jnp.minimum((i + 1) * r, nh - 1), 0)),
            pl.BlockSpec((CONV_WIDTH, CONV_DIM), lambda bi, i: (0, 0)),
            pl.BlockSpec((1, CONV_DIM), lambda bi, i: (0, 0)),
        ],
        out_specs=[
            pl.BlockSpec((1, tm, D_INNER), lambda bi, i: (bi, i, 0)),
            pl.BlockSpec((1, tm, 2 * BC_DIM), lambda bi, i: (bi, i, 0)),
        ],
        out_shape=[jax.ShapeDtypeStruct((b, l, D_INNER), BF16),
                   jax.ShapeDtypeStruct((b, l, 2 * BC_DIM), BF16)],
        scratch_shapes=[pltpu.VMEM((tm + 2 * HALO, CONV_DIM), F32)],
        compiler_params=_cparams("parallel", "parallel"),
        name="conv_silu",
    )(xbc, xbc, xbc, w, bvec)


def _ssd_chunk(x_ref, bc_ref, dt_ref, par_ref, e_ref, y_ref, s_ref, r0, *, fwd):
    rows = lax.broadcasted_iota(jnp.int32, (CHUNK, CHUNK), 0)
    cols = lax.broadcasted_iota(jnp.int32, (CHUNK, CHUNK), 1)
    mask = rows >= cols if fwd else rows <= cols
    low = cols < HALF_LANE
    lane0 = 0 if fwd else N_SSD_HEADS

    x = dt_ref[0, pl.ds(r0, CHUNK), :] + par_ref[0:1, :]
    dt = jnp.maximum(x, 0.0) + jnp.log(1.0 + jnp.exp(-jnp.abs(x)))
    a = dt * (-jnp.exp(par_ref[1:2, :]))
    tri = mask.astype(BF16)
    acs = sum(_dot(tri, part) for part in _split_bf16(a, 3))
    last = acs[CHUNK - 1:CHUNK, :] if fwd else acs[0:1, :]
    acs_t = acs.T
    dt_t = dt.T

    bc = bc_ref[0, pl.ds(r0, CHUNK), :]
    b_all = bc[:, 0:BC_DIM]
    c_all = bc[:, BC_DIM:2 * BC_DIM]
    b_t = b_all.astype(F32).T.astype(BF16)
    zero = jnp.zeros_like(c_all)
    c_grp = [jnp.where(low, c_all, zero), jnp.where(low, zero, c_all)]
    cb = [_dot_nt(c, b_all) for c in c_grp]

    def expand(v):
        return sum(_dot(part, e_ref[...]) for part in _split_bf16(v, 2))

    w_state = expand(dt * jnp.exp(last - acs))
    w_off = expand(jnp.exp(acs))
    decay = expand(jnp.broadcast_to(jnp.exp(last), (8, LANE)))[0:1, :]

    xc = x_ref[0, pl.ds(r0, CHUNK), :]
    xw = (xc.astype(F32) * w_state).astype(BF16)
    for p in range(N_SSD_HEADS // 2):
        g = p // (N_SSD_HEADS // (2 * N_SSD_GROUPS))
        ps = slice(p * LANE, (p + 1) * LANE)
        xp = xc[:, ps]
        ys = []
        for half in range(2):
            lh = lane0 + 2 * p + half
            seg = acs[:, lh:lh + 1] - acs_t[lh:lh + 1, :]
            decay_mat = jnp.exp(jnp.where(mask, seg, MASKED))
            m = (cb[g] * decay_mat * dt_t[lh:lh + 1, :]).astype(BF16)
            ys.append(_dot(m, xp))
        state = s_ref[p]
        y_off = _dot(c_grp[g], state.astype(BF16)) * w_off[:, ps]
        y_ref[0, pl.ds(r0, CHUNK), ps] = (jnp.where(low, ys[0], ys[1]) + y_off).astype(y_ref.dtype)
        s_ref[p] = state * decay[:, ps] + _dot(b_t, xw[:, ps])


def _ssd_kernel(xf_ref, bcf_ref, dtf_ref, xb_ref, bcb_ref, dtb_ref, par_ref, ef_ref, eb_ref,
                yf_ref, yb_ref, sf_ref, sb_ref, *, cps):
    @pl.when(pl.program_id(1) == 0)
    def _():
        sf_ref[...] = jnp.zeros_like(sf_ref)
        sb_ref[...] = jnp.zeros_like(sb_ref)

    def body(j, carry):
        rf = pl.multiple_of(j * CHUNK, CHUNK)
        rb = pl.multiple_of((cps - 1 - j) * CHUNK, CHUNK)
        _ssd_chunk(xf_ref, bcf_ref, dtf_ref, par_ref, ef_ref, yf_ref, sf_ref, rf, fwd=True)
        _ssd_chunk(xb_ref, bcb_ref, dtb_ref, par_ref, eb_ref, yb_ref, sb_ref, rb, fwd=False)
        return carry

    lax.fori_loop(0, cps, body, 0)


def _ssd(xs, bc, dt, par, e_f, e_b, cps):
    b, l, _ = xs.shape
    rows = cps * CHUNK
    ns = l // rows
    fmap = lambda bi, c: (bi, c, 0)
    bmap = lambda bi, c: (bi, ns - 1 - c, 0)
    const = lambda bi, c: (0, 0)
    state = pltpu.VMEM((N_SSD_HEADS // 2, LANE, LANE), F32)
    return pl.pallas_call(
        functools.partial(_ssd_kernel, cps=cps),
        grid=(b, ns),
        in_specs=[
            pl.BlockSpec((1, rows, D_INNER), fmap),
            pl.BlockSpec((1, rows, 2 * BC_DIM), fmap),
            pl.BlockSpec((1, rows, DT_COLS), fmap),
            pl.BlockSpec((1, rows, D_INNER), bmap),
            pl.BlockSpec((1, rows, 2 * BC_DIM), bmap),
            pl.BlockSpec((1, rows, DT_COLS), bmap),
            pl.BlockSpec((8, LANE), const),
            pl.BlockSpec((LANE, D_INNER), const),
            pl.BlockSpec((LANE, D_INNER), const),
        ],
        out_specs=[pl.BlockSpec((1, rows, D_INNER), fmap), pl.BlockSpec((1, rows, D_INNER), bmap)],
        out_shape=[jax.ShapeDtypeStruct((b, l, D_INNER), BF16)] * 2,
        scratch_shapes=[state, state],
        compiler_params=_cparams("parallel", "arbitrary"),
        name="ssd_scan",
    )(xs, bc, dt, xs, bc, dt, par, e_f, e_b)


def _router_gates(lg_t):
    row = lambda j: lg_t[j:j + 1, :]
    lgrp = [row(j) for j in range(N_EGROUPS)]
    m = functools.reduce(jnp.maximum, lgrp)
    g_sel = jnp.where(lgrp[0] == m, 0, jnp.where(lgrp[1] == m, 1, jnp.where(lgrp[2] == m, 2, 3)))
    p_sel = 1.0 / sum(jnp.exp(v - m) for v in lgrp)
    s = []
    for k in range(EXPERTS_PER_GROUP):
        e = [row(N_EGROUPS + EXPERTS_PER_GROUP * g + k) for g in range(N_EGROUPS)]
        s.append(jnp.where(g_sel == 0, e[0], jnp.where(g_sel == 1, e[1], jnp.where(g_sel == 2, e[2], e[3]))))
    m1 = functools.reduce(jnp.maximum, s)
    i1 = jnp.where(s[0] == m1, 0, jnp.where(s[1] == m1, 1, jnp.where(s[2] == m1, 2, 3)))
    s2 = [jnp.where(i1 == k, -jnp.inf, s[k]) for k in range(EXPERTS_PER_GROUP)]
    m2 = functools.reduce(jnp.maximum, s2)
    i2 = jnp.where(s2[0] == m2, 0, jnp.where(s2[1] == m2, 1, jnp.where(s2[2] == m2, 2, 3)))
    t = jnp.exp(m2 - m1)
    w1 = 1.0 / (1.0 + t)
    w2 = t / (1.0 + t)
    tm = lg_t.shape[1]
    rows = lax.broadcasted_iota(jnp.int32, (N_EXPERTS, tm), 0)
    gates = jnp.zeros((N_EXPERTS, tm), F32)
    for g in range(N_EGROUPS):
        for k in range(EXPERTS_PER_GROUP):
            wk = jnp.where(i1 == k, w1, jnp.where(i2 == k, w2, 0.0))
            val = jnp.where(g_sel == g, wk * p_sel, 0.0)
            gates = jnp.where(rows == g * EXPERTS_PER_GROUP + k, val, gates)
    return jnp.concatenate([gates, jnp.zeros((LANE - N_EXPERTS, tm), F32)], axis=0)


def _merge_kernel(yf_ref, yb_ref, xs_ref, z_ref, att_ref, gag_ref, x_ref, dsk_ref, ng_ref,
                  wso_ref, wao_ref, wo_ref, gf_ref, wrh_ref, wrl_ref, rb_ref,
                  x1_ref, h2_ref, gates_ref):
    y = yf_ref[...].astype(F32) + yb_ref[...].astype(F32) + dsk_ref[...] * xs_ref[...].astype(F32)
    z = z_ref[...].astype(F32)
    y = y * (z * _sigmoid(z))
    gsz = D_INNER // N_SSD_GROUPS
    parts = []
    for g in range(N_SSD_GROUPS):
        yg = y[:, g * gsz:(g + 1) * gsz]
        parts.append(yg * lax.rsqrt(jnp.mean(yg * yg, axis=-1, keepdims=True) + EPS))
    ysn = (jnp.concatenate(parts, axis=1) * ng_ref[...]).astype(BF16)
    ys = _dot(ysn, wso_ref[...])
    ya = _dot(att_ref[...], wao_ref[...])
    ga = gag_ref[:, 0:D_MODEL].astype(F32)
    gs = gag_ref[:, D_MODEL:2 * D_MODEL].astype(F32)
    mixed = _sigmoid(ga) * ya + _sigmoid(gs) * ys
    x1 = x_ref[...] + _dot(mixed.astype(BF16), wo_ref[...])
    x1_ref[...] = x1
    h2 = _rmsnorm(x1, gf_ref[...])
    h2_ref[...] = h2.astype(BF16)
    hh, hl = _split_bf16(h2, 2)
    lg = _dot(hh, wrh_ref[...]) + _dot(hl, wrh_ref[...]) + _dot(hh, wrl_ref[...]) + rb_ref[...]
    gates_ref[...] = _router_gates(lg.T).T


def _merge(yf, yb, xs, z, att, gag, x, dsk, ng, wso, wao, wo, gf, wrh, wrl, rb, tm):
    t = x.shape[0]
    tile = lambda n: pl.BlockSpec((tm, n), lambda i: (i, 0))
    full = lambda a: pl.BlockSpec(a.shape, lambda i: (0,) * a.ndim)
    consts = (dsk, ng, wso, wao, wo, gf, wrh, wrl, rb)
    return pl.pallas_call(
        _merge_kernel,
        grid=(t // tm,),
        in_specs=[tile(D_INNER), tile(D_INNER), tile(D_INNER), tile(D_INNER), tile(Q_DIM),
                  tile(GATE_COLS), tile(D_MODEL)] + [full(a) for a in consts],
        out_specs=[tile(D_MODEL), tile(D_MODEL), tile(LANE)],
        out_shape=[jax.ShapeDtypeStruct((t, D_MODEL), F32),
                   jax.ShapeDtypeStruct((t, D_MODEL), BF16),
                   jax.ShapeDtypeStruct((t, LANE), F32)],
        compiler_params=_cparams("parallel"),
        name="merge_router",
    )(yf, yb, xs, z, att, gag, x, *consts)


def _moe_kernel(h_ref, gt_ref, x1_ref, wgu_ref, wd_ref, o_ref):
    e = pl.program_id(1)

    @pl.when(e == 0)
    def _():
        o_ref[...] = x1_ref[...]

    lanes = lax.broadcasted_iota(jnp.int32, gt_ref.shape, 1)
    gcol = jnp.sum(jnp.where(lanes == e, gt_ref[...], 0.0), axis=1, keepdims=True)
    gu = _dot(h_ref[...], wgu_ref[0])
    g = gu[:, 0:D_EXPERT]
    u = gu[:, D_EXPERT:2 * D_EXPERT]
    a = (g * _sigmoid(g)) * u * gcol
    o_ref[...] += _dot(a.astype(BF16), wd_ref[0])


def _moe(h2, gates, x1, wgu, wd, tm):
    t = x1.shape[0]
    return pl.pallas_call(
        _moe_kernel,
        grid=(t // tm, N_EXPERTS),
        in_specs=[
            pl.BlockSpec((tm, D_MODEL), lambda i, e: (i, 0)),
            pl.BlockSpec((tm, LANE), lambda i, e: (i, 0)),
            pl.BlockSpec((tm, D_MODEL), lambda i, e: (i, 0)),
            pl.BlockSpec((1, D_MODEL, 2 * D_EXPERT), lambda i, e: (e, 0, 0)),
            pl.BlockSpec((1, D_EXPERT, D_MODEL), lambda i, e: (e, 0, 0)),
        ],
        out_specs=pl.BlockSpec((tm, D_MODEL), lambda i, e: (i, 0)),
        out_shape=jax.ShapeDtypeStruct((t, D_MODEL), F32),
        compiler_params=_cparams("parallel", "arbitrary"),
        name="moe_experts",
    )(h2, gates, x1, wgu, wd)


def _ple_kernel(x_ref, p_ref, g_ref, wg_ref, wp_ref, o_ref):
    x = x_ref[...]
    gate = _sigmoid(_dot(_rmsnorm(x, g_ref[...]).astype(BF16), wg_ref[...]))
    o_ref[...] = x + gate * _dot(p_ref[...].astype(BF16), wp_ref[...])


def _ple(x, p, g, wg, wp, tm):
    t = x.shape[0]
    return pl.pallas_call(
        _ple_kernel,
        grid=(t // tm,),
        in_specs=[
            pl.BlockSpec((tm, D_MODEL), lambda i: (i, 0)),
            pl.BlockSpec((tm, PLE_DIM), lambda i: (i, 0)),
            pl.BlockSpec((1, D_MODEL), lambda i: (0, 0)),
            pl.BlockSpec((D_MODEL, D_MODEL), lambda i: (0, 0)),
            pl.BlockSpec((PLE_DIM, D_MODEL), lambda i: (0, 0)),
        ],
        out_specs=pl.BlockSpec((tm, D_MODEL), lambda i: (i, 0)),
        out_shape=jax.ShapeDtypeStruct((t, D_MODEL), F32),
        compiler_params=_cparams("parallel"),
        name="ple_gate",
    )(x, p, g, wg, wp)


def _final_norm_kernel(x_ref, g_ref, o_ref):
    o_ref[...] = _rmsnorm(x_ref[...], g_ref[...])


def _final_norm(x, g, tm):
    t = x.shape[0]
    return pl.pallas_call(
        _final_norm_kernel,
        grid=(t // tm,),
        in_specs=[pl.BlockSpec((tm, D_MODEL), lambda i: (i, 0)), pl.BlockSpec((1, D_MODEL), lambda i: (0, 0))],
        out_specs=pl.BlockSpec((tm, D_MODEL), lambda i: (i, 0)),
        out_shape=jax.ShapeDtypeStruct((t, D_MODEL), F32),
        compiler_params=_cparams("parallel"),
        name="final_norm",
    )(x, g)


def _band_bias(rel_bias):
    rel = (np.arange(3 * BLK)[None, :] - BLK) - np.arange(BLK)[:, None]
    nb = N_BUCKETS // 2
    max_exact = nb // 2
    ret = (rel > 0).astype(np.int32) * nb
    n = np.abs(rel)
    large = max_exact + (np.log(np.maximum(n, 1) / max_exact) / np.log(MAX_DISTANCE / max_exact)
                         * (nb - max_exact)).astype(np.int32)
    large = np.minimum(large, nb - 1)
    bucket = (ret + np.where(n < max_exact, n, large)).astype(np.int32)
    bias = jnp.transpose(rel_bias.astype(F32)[jnp.asarray(bucket)], (2, 0, 1))
    return jnp.where(jnp.asarray(n <= WINDOW)[None], bias, MASKED)


def _head_expand_matrix(first_lane):
    e = np.zeros((LANE, D_INNER), np.float32)
    for h in range(N_SSD_HEADS):
        e[first_lane + h, h * SSD_HEAD_DIM:(h + 1) * SSD_HEAD_DIM] = 1.0
    return jnp.asarray(e, BF16)


def _prep_layers(P):
    depth = P["w_in"].shape[0]
    w = P["w_in"]
    o = np.cumsum([0, Q_DIM, KV_DIM, KV_DIM, D_INNER, CONV_DIM, N_SSD_HEADS, N_SSD_HEADS, D_MODEL, D_MODEL])
    q, k, v, z, xbc, dtf, dtb, ga, gs = [w[:, :, o[i]:o[i + 1]] for i in range(9)]
    dup = lambda a: jnp.concatenate([a[..., :HEAD_DIM]] * 2 + [a[..., HEAD_DIM:]] * 2, axis=-1)
    dt_pad = jnp.zeros((depth, D_MODEL, DT_COLS - 2 * N_SSD_HEADS), F32)
    w_in = jnp.concatenate([q * (HEAD_DIM ** -0.5), dup(k), dup(v), z, xbc, ga, gs, dtf, dtb, dt_pad],
                           axis=-1).astype(BF16)

    pad_lanes = lambda a: jnp.pad(a, ((0, 0), (0, LANE - a.shape[-1])))
    par = jnp.stack([pad_lanes(jnp.concatenate([P["dt_bias_f"], P["dt_bias_b"]], axis=-1)),
                     pad_lanes(jnp.concatenate([P["a_log_f"], P["a_log_b"]], axis=-1))], axis=1)
    par = jnp.pad(par.astype(F32), ((0, 0), (0, 6), (0, 0)))

    wr = jnp.concatenate([P["router_g"], P["router_e"]], axis=-1)
    wr = jnp.pad(wr, ((0, 0), (0, 0), (0, LANE - wr.shape[-1])))
    wr_hi = wr.astype(BF16)
    wr_lo = (wr - wr_hi.astype(F32)).astype(BF16)
    rb = pad_lanes(jnp.concatenate([P["router_g_b"], P["router_e_b"]], axis=-1))[:, None, :]

    row = lambda a: a.astype(F32)[:, None, :]
    return dict(
        g_mix=row(P["g_mix"]), w_in=w_in, sink=P["attn_sink"].astype(F32),
        w_attn_out=P["w_attn_out"].astype(BF16),
        conv_w=P["conv_w"].astype(F32), conv_b=row(P["conv_b"]), par=par,
        d_skip=row(jnp.repeat(P["d_skip"], SSD_HEAD_DIM, axis=-1)), ssd_norm_g=row(P["ssd_norm_g"]),
        w_ssd_out=P["w_ssd_out"].astype(BF16), w_out=P["w_out"].astype(BF16),
        g_ffn=row(P["g_ffn"]), wr_hi=wr_hi, wr_lo=wr_lo, rb=rb.astype(F32),
        w_gate_up=P["w_gate_up"].astype(BF16), w_down=P["w_down"].astype(BF16),
        g_ple=row(P["g_ple"]), w_ple_gate=P["w_ple_gate"].astype(BF16),
        w_ple_proj=P["w_ple_proj"].astype(BF16),
    )


def _tile(n, want):
    t = min(n, want)
    assert n % t == 0
    return t


def _trunk(x, p, layers, bias, e_f, e_b, g_final):
    b, l, _ = x.shape
    t = b * l
    tm = _tile(t, 512)
    seq_tile = _tile(l, 512)
    cps = seq_tile // CHUNK

    def layer(xf, lw):
        W, pl_i = lw
        qkv, z, xbc, gag, dt = _inproj(xf, W["g_mix"], W["w_in"], tm)
        att = _attention(qkv.reshape(b, l, QKV_COLS), W["sink"], bias, seq_tile)
        xs, bc = _conv(xbc.reshape(b, l, CONV_DIM), W["conv_w"], W["conv_b"], seq_tile)
        yf, yb = _ssd(xs, bc, dt.reshape(b, l, DT_COLS), W["par"], e_f, e_b, cps)
        x1, h2, gates = _merge(yf.reshape(t, D_INNER), yb.reshape(t, D_INNER), xs.reshape(t, D_INNER), z,
                               att.reshape(t, Q_DIM), gag, xf, W["d_skip"], W["ssd_norm_g"], W["w_ssd_out"],
                               W["w_attn_out"], W["w_out"], W["g_ffn"], W["wr_hi"], W["wr_lo"], W["rb"], tm)
        x2 = _moe(h2, gates, x1, W["w_gate_up"], W["w_down"], tm)
        x3 = _ple(x2, pl_i, W["g_ple"], W["w_ple_gate"], W["w_ple_proj"], tm)
        return x3, None

    depth = p.shape[0]
    xf, _ = lax.scan(layer, x.reshape(t, D_MODEL), (layers, p.reshape(depth, t, PLE_DIM)))
    return _final_norm(xf, g_final.astype(F32)[None, :], tm).reshape(b, l, D_MODEL)


def kernel(x_prompt, x_sample, p_prompt, p_sample, g_mix, w_in, rel_bias, attn_sink, w_attn_out, conv_w, conv_b, a_log_f, a_log_b, dt_bias_f, dt_bias_b, d_skip, ssd_norm_g, w_ssd_out, w_out, g_ffn, router_g, router_g_b, router_e, router_e_b, w_gate_up, w_down, g_ple, w_ple_gate, w_ple_proj, g_final):
    P = dict(g_mix=g_mix, w_in=w_in, attn_sink=attn_sink, w_attn_out=w_attn_out, conv_w=conv_w, conv_b=conv_b,
             a_log_f=a_log_f, a_log_b=a_log_b, dt_bias_f=dt_bias_f, dt_bias_b=dt_bias_b, d_skip=d_skip,
             ssd_norm_g=ssd_norm_g, w_ssd_out=w_ssd_out, w_out=w_out, g_ffn=g_ffn, router_g=router_g,
             router_g_b=router_g_b, router_e=router_e, router_e_b=router_e_b, w_gate_up=w_gate_up,
             w_down=w_down, g_ple=g_ple, w_ple_gate=w_ple_gate, w_ple_proj=w_ple_proj)
    layers = _prep_layers(P)
    bias = _band_bias(rel_bias)
    e_f = _head_expand_matrix(0)
    e_b = _head_expand_matrix(N_SSD_HEADS)
    y_prompt = _trunk(x_prompt, p_prompt, layers, bias, e_f, e_b, g_final)
    y_sample = _trunk(x_sample, p_sample, layers, bias, e_f, e_b, g_final)
    return (y_prompt, y_sample)
```

```python
import functools

import numpy as np
import jax
import jax.numpy as jnp
from jax import lax
from jax.experimental import pallas as pl
from jax.experimental.pallas import tpu as pltpu

F32 = jnp.float32
BF16 = jnp.bfloat16

LANE = 128
HALF_LANE = LANE // 2
VMEM_LIMIT_BYTES = 56 * 1024 * 1024

D_MODEL = 1024
HEAD_DIM = 64
N_HEADS = 8
N_KV_HEADS = 2
GQ = N_HEADS // N_KV_HEADS
Q_DIM = N_HEADS * HEAD_DIM
KV_DIM = N_KV_HEADS * HEAD_DIM
WINDOW = 128
BLK = 128
N_BUCKETS = 32
MAX_DISTANCE = 128
D_INNER = 1024
SSD_HEAD_DIM = 64
N_SSD_HEADS = D_INNER // SSD_HEAD_DIM
N_SSD_GROUPS = 2
D_STATE = 64
CHUNK = 128
CONV_WIDTH = 5
CONV_PAD = CONV_WIDTH // 2
BC_DIM = N_SSD_GROUPS * D_STATE
CONV_DIM = D_INNER + 2 * BC_DIM
N_EGROUPS = 4
EXPERTS_PER_GROUP = 4
N_EXPERTS = N_EGROUPS * EXPERTS_PER_GROUP
D_EXPERT = 256
PLE_DIM = 256
EPS = 1e-6
MASKED = -1e30

QKV_COLS = Q_DIM + 2 * 2 * KV_DIM
GATE_COLS = 2 * D_MODEL
DT_COLS = LANE
IN_COLS_PADDED = QKV_COLS + D_INNER + CONV_DIM + GATE_COLS + DT_COLS
HALO = 16
ROUTED_WORDS = D_MODEL // 2


def _cparams(*sem):
    return pltpu.CompilerParams(dimension_semantics=sem, vmem_limit_bytes=VMEM_LIMIT_BYTES)


def _dot(a, b):
    return jnp.dot(a, b, preferred_element_type=F32)


def _dot_nt(a, b):
    return lax.dot_general(a, b, (((1,), (1,)), ((), ())), preferred_element_type=F32)


def _split_bf16(v, terms):
    parts = []
    r = v
    for _ in range(terms):
        p = r.astype(BF16)
        parts.append(p)
        r = r - p.astype(F32)
    return parts


def _rmsnorm(x, g):
    return x * lax.rsqrt(jnp.mean(x * x, axis=-1, keepdims=True) + EPS) * g


def _sigmoid(x):
    return 1.0 / (1.0 + jnp.exp(-x))


def _inproj_kernel(x_ref, g_ref, w_ref, qkv_ref, z_ref, xbc_ref, gag_ref, dt_ref):
    hb = _rmsnorm(x_ref[...], g_ref[...]).astype(BF16)
    off = 0
    for ref in (qkv_ref, z_ref, xbc_ref, gag_ref, dt_ref):
        n = ref.shape[-1]
        ref[...] = _dot(hb, w_ref[:, off:off + n]).astype(ref.dtype)
        off += n


def _inproj(x, g, w, tm):
    t = x.shape[0]
    widths = (QKV_COLS, D_INNER, CONV_DIM, GATE_COLS, DT_COLS)
    dtypes = (BF16, BF16, BF16, BF16, F32)
    return pl.pallas_call(
        _inproj_kernel,
        grid=(t // tm,),
        in_specs=[
            pl.BlockSpec((tm, D_MODEL), lambda i: (i, 0)),
            pl.BlockSpec((1, D_MODEL), lambda i: (0, 0)),
            pl.BlockSpec((D_MODEL, IN_COLS_PADDED), lambda i: (0, 0)),
        ],
        out_specs=[pl.BlockSpec((tm, n), lambda i: (i, 0)) for n in widths],
        out_shape=[jax.ShapeDtypeStruct((t, n), d) for n, d in zip(widths, dtypes)],
        compiler_params=_cparams("parallel"),
        name="inproj",
    )(x, g, w)


def _attn_kernel(sink_ref, q_ref, kp_ref, km_ref, kn_ref, vp_ref, vm_ref, vn_ref, bias_ref, o_ref,
                 k_ext, v_ext, *, qb, seq):
    n = pl.program_id(1)
    k_ext[0:BLK] = kp_ref[0]
    k_ext[BLK:BLK + qb] = km_ref[0]
    k_ext[BLK + qb:] = kn_ref[0]
    v_ext[0:BLK] = vp_ref[0]
    v_ext[BLK:BLK + qb] = vm_ref[0]
    v_ext[BLK + qb:] = vn_ref[0]
    low = lax.broadcasted_iota(jnp.int32, (BLK, LANE), 1) < HALF_LANE
    kcol = lax.broadcasted_iota(jnp.int32, (1, 3 * BLK), 1)

    def sub_block(j, carry):
        r0 = pl.multiple_of(j * BLK, BLK)
        kpos = n * qb + r0 - BLK + kcol
        valid = (kpos >= 0) & (kpos < seq)
        k3 = k_ext[pl.ds(r0, 3 * BLK), :]
        v3 = v_ext[pl.ds(r0, 3 * BLK), :]
        for pair in range(N_HEADS // 2):
            kv = pair // (GQ // 2)
            qp = q_ref[0, pl.ds(r0, BLK), pair * LANE:(pair + 1) * LANE]
            kk = k3[:, kv * LANE:(kv + 1) * LANE]
            vv = v3[:, kv * LANE:(kv + 1) * LANE]
            outs = []
            for half in range(2):
                hq = 2 * pair + half
                qm = jnp.where(low if half == 0 else jnp.logical_not(low), qp, jnp.zeros_like(qp))
                s = _dot_nt(qm, kk) + bias_ref[hq]
                s = jnp.where(valid, s, MASKED)
                sk = sink_ref[hq]
                m = jnp.maximum(jnp.max(s, axis=-1, keepdims=True), sk)
                p = jnp.exp(s - m)
                den = jnp.sum(p, axis=-1, keepdims=True) + jnp.exp(sk - m)
                outs.append(_dot(p.astype(BF16), vv) / den)
            o_ref[0, pl.ds(r0, BLK), pair * LANE:(pair + 1) * LANE] = (
                jnp.where(low, outs[0], outs[1]).astype(o_ref.dtype))
        return carry

    lax.fori_loop(0, qb // BLK, sub_block, 0)


def _attention(qkv, sink, bias, qb):
    b, l, _ = qkv.shape
    nq = l // qb
    r = qb // BLK
    nblk = l // BLK
    kcol, vcol = Q_DIM // (2 * LANE), Q_DIM // (2 * LANE) + 1

    def prev_map(c):
        return lambda bi, n: (bi, jnp.maximum(n * r - 1, 0), c)

    def next_map(c):
        return lambda bi, n: (bi, jnp.minimum((n + 1) * r, nblk - 1), c)

    kvw = 2 * LANE
    return pl.pallas_call(
        functools.partial(_attn_kernel, qb=qb, seq=l),
        grid=(b, nq),
        in_specs=[
            pl.BlockSpec(memory_space=pltpu.SMEM),
            pl.BlockSpec((1, qb, Q_DIM), lambda bi, n: (bi, n, 0)),
            pl.BlockSpec((1, BLK, kvw), prev_map(kcol)),
            pl.BlockSpec((1, qb, kvw), lambda bi, n: (bi, n, kcol)),
            pl.BlockSpec((1, BLK, kvw), next_map(kcol)),
            pl.BlockSpec((1, BLK, kvw), prev_map(vcol)),
            pl.BlockSpec((1, qb, kvw), lambda bi, n: (bi, n, vcol)),
            pl.BlockSpec((1, BLK, kvw), next_map(vcol)),
            pl.BlockSpec((N_HEADS, BLK, 3 * BLK), lambda bi, n: (0, 0, 0)),
        ],
        out_specs=pl.BlockSpec((1, qb, Q_DIM), lambda bi, n: (bi, n, 0)),
        out_shape=jax.ShapeDtypeStruct((b, l, Q_DIM), BF16),
        scratch_shapes=[pltpu.VMEM((qb + 2 * BLK, kvw), BF16), pltpu.VMEM((qb + 2 * BLK, kvw), BF16)],
        compiler_params=_cparams("parallel", "parallel"),
        name="attention",
    )(sink, qkv, qkv, qkv, qkv, qkv, qkv, qkv, bias)


def _conv_kernel(p_ref, m_ref, n_ref, w_ref, b_ref, xs_ref, bc_ref, ext, *, tm, nblk):
    i = pl.program_id(1)
    ext[0:HALO] = jnp.where(i > 0, p_ref[0].astype(F32), 0.0)
    ext[HALO:HALO + tm] = m_ref[0].astype(F32)
    ext[HALO + tm:] = jnp.where(i < nblk - 1, n_ref[0].astype(F32), 0.0)
    for cb in range(CONV_DIM // LANE):
        cs = slice(cb * LANE, (cb + 1) * LANE)
        acc = b_ref[:, cs] + w_ref[0:1, cs] * ext[HALO - CONV_PAD:HALO - CONV_PAD + tm, cs]
        for k in range(1, CONV_WIDTH):
            acc = acc + w_ref[k:k + 1, cs] * ext[HALO - CONV_PAD + k:HALO - CONV_PAD + k + tm, cs]
        y = (acc * _sigmoid(acc))
        if cb < D_INNER // LANE:
            xs_ref[0, :, cs] = y.astype(xs_ref.dtype)
        else:
            c0 = cb * LANE - D_INNER
            bc_ref[0, :, c0:c0 + LANE] = y.astype(bc_ref.dtype)


def _conv(xbc, w, bvec, tm):
    b, l, _ = xbc.shape
    nblk = l // tm
    r = tm // HALO
    nh = l // HALO
    return pl.pallas_call(
        functools.partial(_conv_kernel, tm=tm, nblk=nblk),
        grid=(b, nblk),
        in_specs=[
            pl.BlockSpec((1, HALO, CONV_DIM), lambda bi, i: (bi, jnp.maximum(i * r - 1, 0), 0)),
            pl.BlockSpec((1, tm, CONV_DIM), lambda bi, i: (bi, i, 0)),
            pl.BlockSpec((1, HALO, CONV_DIM), lambda bi, i: (bi, jnp.minimum((i + 1) * r, nh - 1), 0)),
            pl.BlockSpec((CONV_WIDTH, CONV_DIM), lambda bi, i: (0, 0)),
            pl.BlockSpec((1, CONV_DIM), lambda bi, i: (0, 0)),
        ],
        out_specs=[
            pl.BlockSpec((1, tm, D_INNER), lambda bi, i: (bi, i, 0)),
            pl.BlockSpec((1, tm, 2 * BC_DIM), lambda bi, i: (bi, i, 0)),
        ],
        out_shape=[jax.ShapeDtypeStruct((b, l, D_INNER), BF16),
                   jax.ShapeDtypeStruct((b, l, 2 * BC_DIM), BF16)],
        scratch_shapes=[pltpu.VMEM((tm + 2 * HALO, CONV_DIM), F32)],
        compiler_params=_cparams("parallel", "parallel"),
        name="conv_silu",
    )(xbc, xbc, xbc, w, bvec)


def _ssd_chunk(x_ref, bc_ref, dt_ref, par_ref, e_ref, y_ref, s_ref, r0, *, fwd):
    rows = lax.broadcasted_iota(jnp.int32, (CHUNK, CHUNK), 0)
    cols = lax.broadcasted_iota(jnp.int32, (CHUNK, CHUNK), 1)
    mask = rows >= cols if fwd else rows <= cols
    low = cols < HALF_LANE
    lane0 = 0 if fwd else N_SSD_HEADS

    x = dt_ref[0, pl.ds(r0, CHUNK), :] + par_ref[0:1, :]
    dt = jnp.maximum(x, 0.0) + jnp.log(1.0 + jnp.exp(-jnp.abs(x)))
    a = dt * (-jnp.exp(par_ref[1:2, :]))
    tri = mask.astype(BF16)
    acs = sum(_dot(tri, part) for part in _split_bf16(a, 3))
    last = acs[CHUNK - 1:CHUNK, :] if fwd else acs[0:1, :]
    acs_t = acs.T
    dt_t = dt.T

    bc = bc_ref[0, pl.ds(r0, CHUNK), :]
    b_all = bc[:, 0:BC_DIM]
    c_all = bc[:, BC_DIM:2 * BC_DIM]
    b_t = b_all.astype(F32).T.astype(BF16)
    zero = jnp.zeros_like(c_all)
    c_grp = [jnp.where(low, c_all, zero), jnp.where(low, zero, c_all)]
    cb = [_dot_nt(c, b_all) for c in c_grp]

    def expand(v):
        return sum(_dot(part, e_ref[...]) for part in _split_bf16(v, 2))

    w_state = expand(dt * jnp.exp(last - acs))
    w_off = expand(jnp.exp(acs))
    decay = expand(jnp.broadcast_to(jnp.exp(last), (8, LANE)))[0:1, :]

    xc = x_ref[0, pl.ds(r0, CHUNK), :]
    xw = (xc.astype(F32) * w_state).astype(BF16)
    for p in range(N_SSD_HEADS // 2):
        g = p // (N_SSD_HEADS // (2 * N_SSD_GROUPS))
        ps = slice(p * LANE, (p + 1) * LANE)
        xp = xc[:, ps]
        ys = []
        for half in range(2):
            lh = lane0 + 2 * p + half
            seg = acs[:, lh:lh + 1] - acs_t[lh:lh + 1, :]
            decay_mat = jnp.exp(jnp.where(mask, seg, MASKED))
            m = (cb[g] * decay_mat * dt_t[lh:lh + 1, :]).astype(BF16)
            ys.append(_dot(m, xp))
        state = s_ref[p]
        y_off = _dot(c_grp[g], state.astype(BF16)) * w_off[:, ps]
        y_ref[0, pl.ds(r0, CHUNK), ps] = (jnp.where(low, ys[0], ys[1]) + y_off).astype(y_ref.dtype)
        s_ref[p] = state * decay[:, ps] + _dot(b_t, xw[:, ps])


def _ssd_kernel(xf_ref, bcf_ref, dtf_ref, xb_ref, bcb_ref, dtb_ref, par_ref, ef_ref, eb_ref,
                yf_ref, yb_ref, sf_ref, sb_ref, *, cps):
    @pl.when(pl.program_id(1) == 0)
    def _():
        sf_ref[...] = jnp.zeros_like(sf_ref)
        sb_ref[...] = jnp.zeros_like(sb_ref)

    def body(j, carry):
        rf = pl.multiple_of(j * CHUNK, CHUNK)
        rb = pl.multiple_of((cps - 1 - j) * CHUNK, CHUNK)
        _ssd_chunk(xf_ref, bcf_ref, dtf_ref, par_ref, ef_ref, yf_ref, sf_ref, rf, fwd=True)
        _ssd_chunk(xb_ref, bcb_ref, dtb_ref, par_ref, eb_ref, yb_ref, sb_ref, rb, fwd=False)
        return carry

    lax.fori_loop(0, cps, body, 0)


def _ssd(xs, bc, dt, par, e_f, e_b, cps):
    b, l, _ = xs.shape
    rows = cps * CHUNK
    ns = l // rows
    fmap = lambda bi, c: (bi, c, 0)
    bmap = lambda bi, c: (bi, ns - 1 - c, 0)
    const = lambda bi, c: (0, 0)
    state = pltpu.VMEM((N_SSD_HEADS // 2, LANE, LANE), F32)
    return pl.pallas_call(
        functools.partial(_ssd_kernel, cps=cps),
        grid=(b, ns),
        in_specs=[
            pl.BlockSpec((1, rows, D_INNER), fmap),
            pl.BlockSpec((1, rows, 2 * BC_DIM), fmap),
            pl.BlockSpec((1, rows, DT_COLS), fmap),
            pl.BlockSpec((1, rows, D_INNER), bmap),
            pl.BlockSpec((1, rows, 2 * BC_DIM), bmap),
            pl.BlockSpec((1, rows, DT_COLS), bmap),
            pl.BlockSpec((8, LANE), const),
            pl.BlockSpec((LANE, D_INNER), const),
            pl.BlockSpec((LANE, D_INNER), const),
        ],
        out_specs=[pl.BlockSpec((1, rows, D_INNER), fmap), pl.BlockSpec((1, rows, D_INNER), bmap)],
        out_shape=[jax.ShapeDtypeStruct((b, l, D_INNER), BF16)] * 2,
        scratch_shapes=[state, state],
        compiler_params=_cparams("parallel", "arbitrary"),
        name="ssd_scan",
    )(xs, bc, dt, xs, bc, dt, par, e_f, e_b)


def _router_gates(lg_t):
    row = lambda j: lg_t[j:j + 1, :]
    lgrp = [row(j) for j in range(N_EGROUPS)]
    m = functools.reduce(jnp.maximum, lgrp)
    g_sel = jnp.where(lgrp[0] == m, 0, jnp.where(lgrp[1] == m, 1, jnp.where(lgrp[2] == m, 2, 3)))
    p_sel = 1.0 / sum(jnp.exp(v - m) for v in lgrp)
    s = []
    for k in range(EXPERTS_PER_GROUP):
        e = [row(N_EGROUPS + EXPERTS_PER_GROUP * g + k) for g in range(N_EGROUPS)]
        s.append(jnp.where(g_sel == 0, e[0], jnp.where(g_sel == 1, e[1], jnp.where(g_sel == 2, e[2], e[3]))))
    m1 = functools.reduce(jnp.maximum, s)
    i1 = jnp.where(s[0] == m1, 0, jnp.where(s[1] == m1, 1, jnp.where(s[2] == m1, 2, 3)))
    s2 = [jnp.where(i1 == k, -jnp.inf, s[k]) for k in range(EXPERTS_PER_GROUP)]
    m2 = functools.reduce(jnp.maximum, s2)
    i2 = jnp.where(s2[0] == m2, 0, jnp.where(s2[1] == m2, 1, jnp.where(s2[2] == m2, 2, 3)))
    t = jnp.exp(m2 - m1)
    w1 = 1.0 / (1.0 + t)
    w2 = t / (1.0 + t)
    tm = lg_t.shape[1]
    rows = lax.broadcasted_iota(jnp.int32, (N_EXPERTS, tm), 0)
    gates = jnp.zeros((N_EXPERTS, tm), F32)
    for g in range(N_EGROUPS):
        for k in range(EXPERTS_PER_GROUP):
            wk = jnp.where(i1 == k, w1, jnp.where(i2 == k, w2, 0.0))
            val = jnp.where(g_sel == g, wk * p_sel, 0.0)
            gates = jnp.where(rows == g * EXPERTS_PER_GROUP + k, val, gates)
    return jnp.concatenate([gates, jnp.zeros((LANE - N_EXPERTS, tm), F32)], axis=0), g_sel


def _pack_bf16_pairs(h):
    bits = lax.bitcast_convert_type(h.astype(BF16).astype(F32), jnp.uint32)
    half = h.shape[1] // 2
    return (bits[:, :half] >> 16) | (bits[:, half:] & jnp.uint32(0xFFFF0000))


def _unpack_bf16_pairs(w):
    lo = lax.bitcast_convert_type(w << 16, F32).astype(BF16)
    hi = lax.bitcast_convert_type(w & jnp.uint32(0xFFFF0000), F32).astype(BF16)
    return lo, hi


def _merge_kernel(yf_ref, yb_ref, xs_ref, z_ref, att_ref, gag_ref, x_ref, dsk_ref, ng_ref,
                  wso_ref, wao_ref, wo_ref, gf_ref, wrh_ref, wrl_ref, rb_ref, upper_ref,
                  x1_ref, h2g_ref, grp_ref, rank_ref, counts_ref, cnt):
    @pl.when(pl.program_id(0) == 0)
    def _():
        cnt[...] = jnp.zeros_like(cnt)

    y = yf_ref[...].astype(F32) + yb_ref[...].astype(F32) + dsk_ref[...] * xs_ref[...].astype(F32)
    z = z_ref[...].astype(F32)
    y = y * (z * _sigmoid(z))
    gsz = D_INNER // N_SSD_GROUPS
    parts = []
    for g in range(N_SSD_GROUPS):
        yg = y[:, g * gsz:(g + 1) * gsz]
        parts.append(yg * lax.rsqrt(jnp.mean(yg * yg, axis=-1, keepdims=True) + EPS))
    ysn = (jnp.concatenate(parts, axis=1) * ng_ref[...]).astype(BF16)
    ys = _dot(ysn, wso_ref[...])
    ya = _dot(att_ref[...], wao_ref[...])
    ga = gag_ref[:, 0:D_MODEL].astype(F32)
    gs = gag_ref[:, D_MODEL:2 * D_MODEL].astype(F32)
    mixed = _sigmoid(ga) * ya + _sigmoid(gs) * ys
    x1 = x_ref[...] + _dot(mixed.astype(BF16), wo_ref[...])
    x1_ref[...] = x1
    h2 = _rmsnorm(x1, gf_ref[...])
    hh, hl = _split_bf16(h2, 2)
    lg = _dot(hh, wrh_ref[...]) + _dot(hl, wrh_ref[...]) + _dot(hh, wrl_ref[...]) + rb_ref[...]
    gates_t, g_sel = _router_gates(lg.T)
    h2g_ref[:, 0:ROUTED_WORDS] = _pack_bf16_pairs(h2)
    h2g_ref[:, ROUTED_WORDS:] = lax.bitcast_convert_type(gates_t.T, jnp.uint32)
    tm = x1.shape[0]
    onehot = (lax.broadcasted_iota(jnp.int32, (8, tm), 0) == g_sel).astype(F32)
    rank_all = _dot(onehot.astype(BF16), upper_ref[...]) + cnt[:, 0:1]
    rank_ref[0] = jnp.sum(onehot * rank_all, axis=0, keepdims=True).astype(jnp.int32)
    grp_ref[0] = g_sel
    cnt[...] += jnp.sum(onehot, axis=1, keepdims=True)
    counts_ref[...] = cnt[...]


def _merge(yf, yb, xs, z, att, gag, x, dsk, ng, wso, wao, wo, gf, wrh, wrl, rb, tm):
    t = x.shape[0]
    nt = t // tm
    upper = jnp.asarray(np.triu(np.ones((tm, tm), np.float32), 1), BF16)
    tile = lambda n: pl.BlockSpec((tm, n), lambda i: (i, 0))
    full = lambda a: pl.BlockSpec(a.shape, lambda i: (0,) * a.ndim)
    row = pl.BlockSpec((1, 1, tm), lambda i: (i, 0, 0))
    consts = (dsk, ng, wso, wao, wo, gf, wrh, wrl, rb, upper)
    return pl.pallas_call(
        _merge_kernel,
        grid=(nt,),
        in_specs=[tile(D_INNER), tile(D_INNER), tile(D_INNER), tile(D_INNER), tile(Q_DIM),
                  tile(GATE_COLS), tile(D_MODEL)] + [full(a) for a in consts],
        out_specs=[tile(D_MODEL), tile(ROUTED_WORDS + LANE), row, row, pl.BlockSpec((8, LANE), lambda i: (0, 0))],
        out_shape=[jax.ShapeDtypeStruct((t, D_MODEL), F32),
                   jax.ShapeDtypeStruct((t, ROUTED_WORDS + LANE), jnp.uint32),
                   jax.ShapeDtypeStruct((nt, 1, tm), jnp.int32),
                   jax.ShapeDtypeStruct((nt, 1, tm), jnp.int32),
                   jax.ShapeDtypeStruct((8, LANE), F32)],
        scratch_shapes=[pltpu.VMEM((8, LANE), F32)],
        compiler_params=_cparams("arbitrary"),
        name="merge_router",
    )(yf, yb, xs, z, att, gag, x, *consts)


def _dma_params(*sem):
    return pltpu.CompilerParams(dimension_semantics=sem, vmem_limit_bytes=VMEM_LIMIT_BYTES,
                                disable_bounds_checks=True)


def _permute_kernel(pos_ref, src_ref, init_ref, dst_ref, sem, *, rows):
    del init_ref
    base = pl.program_id(0) * rows

    def issue(r, carry):
        pltpu.make_async_copy(src_ref.at[pl.ds(base + r, 1)], dst_ref.at[pl.ds(pos_ref[0, 0, r], 1)], sem).start()
        return carry

    lax.fori_loop(0, rows, issue, 0, unroll=8)

    def drain(r, carry):
        pltpu.make_async_copy(src_ref.at[pl.ds(0, 1)], dst_ref.at[pl.ds(0, 1)], sem).wait()
        return carry

    lax.fori_loop(0, rows, drain, 0, unroll=8)


def _permute(h2g, pos, t_pad, rows):
    t, w = h2g.shape
    nt = t // rows
    return pl.pallas_call(
        functools.partial(_permute_kernel, rows=rows),
        grid=(nt,),
        in_specs=[pl.BlockSpec((1, 1, rows), lambda i: (i, 0, 0), memory_space=pltpu.SMEM),
                  pl.BlockSpec(memory_space=pl.ANY),
                  pl.BlockSpec(memory_space=pl.ANY)],
        out_specs=pl.BlockSpec(memory_space=pl.ANY),
        out_shape=jax.ShapeDtypeStruct((t_pad, w), h2g.dtype),
        scratch_shapes=[pltpu.SemaphoreType.DMA(())],
        input_output_aliases={2: 0},
        compiler_params=_dma_params("arbitrary"),
        name="moe_permute",
    )(pos.reshape(nt, 1, rows), h2g, jnp.zeros((t_pad, w), h2g.dtype))


def _experts_kernel(tg_ref, tv_ref, hs_ref, wgu_ref, wd_ref, o_ref):
    i = pl.program_id(0)

    @pl.when(tv_ref[i] == 0)
    def _():
        o_ref[...] = jnp.zeros_like(o_ref)

    @pl.when(tv_ref[i] != 0)
    def _():
        lo, hi = _unpack_bf16_pairs(hs_ref[:, 0:ROUTED_WORDS])
        gates = lax.bitcast_convert_type(hs_ref[:, ROUTED_WORDS:], F32)
        lanes = lax.broadcasted_iota(jnp.int32, gates.shape, 1)
        first = tg_ref[i] * EXPERTS_PER_GROUP
        acc = jnp.zeros(o_ref.shape, F32)
        for k in range(EXPERTS_PER_GROUP):
            gcol = jnp.sum(jnp.where(lanes == first + k, gates, 0.0), axis=1, keepdims=True)
            gu = _dot(lo, wgu_ref[0, k, 0:ROUTED_WORDS, :]) + _dot(hi, wgu_ref[0, k, ROUTED_WORDS:, :])
            g = gu[:, 0:D_EXPERT]
            u = gu[:, D_EXPERT:2 * D_EXPERT]
            a = (g * _sigmoid(g)) * u * gcol
            acc = acc + _dot(a.astype(BF16), wd_ref[0, k])
        o_ref[...] = acc


def _experts(hs, tile_group, tile_valid, wgu, wd, tms):
    t_pad, w = hs.shape
    wgu = wgu.reshape(N_EGROUPS, EXPERTS_PER_GROUP, D_MODEL, 2 * D_EXPERT)
    wd = wd.reshape(N_EGROUPS, EXPERTS_PER_GROUP, D_EXPERT, D_MODEL)
    return pl.pallas_call(
        _experts_kernel,
        grid_spec=pltpu.PrefetchScalarGridSpec(
            num_scalar_prefetch=2,
            grid=(t_pad // tms,),
            in_specs=[
                pl.BlockSpec((tms, w), lambda i, tg, tv: (i, 0)),
                pl.BlockSpec((1, EXPERTS_PER_GROUP, D_MODEL, 2 * D_EXPERT), lambda i, tg, tv: (tg[i], 0, 0, 0)),
                pl.BlockSpec((1, EXPERTS_PER_GROUP, D_EXPERT, D_MODEL), lambda i, tg, tv: (tg[i], 0, 0, 0)),
            ],
            out_specs=pl.BlockSpec((tms, D_MODEL), lambda i, tg, tv: (i, 0)),
        ),
        out_shape=jax.ShapeDtypeStruct((t_pad, D_MODEL), F32),
        compiler_params=_cparams("arbitrary"),
        name="moe_experts",
    )(tile_group, tile_valid, hs, wgu, wd)


def _ple_kernel(pos_ref, posn_ref, x_ref, p_ref, g_ref, wg_ref, wp_ref, ys_ref, o_ref, buf, sem, *, tm, nt):
    i = pl.program_id(0)
    slot = i % 2

    def gather(pref, s):
        def issue(r, carry):
            pltpu.make_async_copy(ys_ref.at[pl.ds(pref[0, 0, r], 1)], buf.at[s, pl.ds(r, 1)], sem.at[s]).start()
            return carry
        lax.fori_loop(0, tm, issue, 0, unroll=8)

    @pl.when(i == 0)
    def _():
        gather(pos_ref, 0)

    @pl.when(i + 1 < nt)
    def _():
        gather(posn_ref, 1 - slot)

    def drain(r, carry):
        pltpu.make_async_copy(ys_ref.at[pl.ds(0, 1)], buf.at[slot, pl.ds(0, 1)], sem.at[slot]).wait()
        return carry

    lax.fori_loop(0, tm, drain, 0, unroll=8)
    x = x_ref[...] + buf[slot]
    gate = _sigmoid(_dot(_rmsnorm(x, g_ref[...]).astype(BF16), wg_ref[...]))
    o_ref[...] = x + gate * _dot(p_ref[...].astype(BF16), wp_ref[...])


def _ple(x, ys, pos, p, g, wg, wp, tm):
    t = x.shape[0]
    nt = t // tm
    pos3 = pos.reshape(nt, 1, tm)
    return pl.pallas_call(
        functools.partial(_ple_kernel, tm=tm, nt=nt),
        grid=(nt,),
        in_specs=[
            pl.BlockSpec((1, 1, tm), lambda i: (i, 0, 0), memory_space=pltpu.SMEM),
            pl.BlockSpec((1, 1, tm), lambda i: (jnp.minimum(i + 1, nt - 1), 0, 0), memory_space=pltpu.SMEM),
            pl.BlockSpec((tm, D_MODEL), lambda i: (i, 0)),
            pl.BlockSpec((tm, PLE_DIM), lambda i: (i, 0)),
            pl.BlockSpec((1, D_MODEL), lambda i: (0, 0)),
            pl.BlockSpec((D_MODEL, D_MODEL), lambda i: (0, 0)),
            pl.BlockSpec((PLE_DIM, D_MODEL), lambda i: (0, 0)),
            pl.BlockSpec(memory_space=pl.ANY),
        ],
        out_specs=pl.BlockSpec((tm, D_MODEL), lambda i: (i, 0)),
        out_shape=jax.ShapeDtypeStruct((t, D_MODEL), F32),
        scratch_shapes=[pltpu.VMEM((2, tm, D_MODEL), F32), pltpu.SemaphoreType.DMA((2,))],
        compiler_params=_dma_params("arbitrary"),
        name="ple_gate",
    )(pos3, pos3, x, p, g, wg, wp, ys)


def _route(grp, rank, counts, t, tms):
    cnt = counts[:N_EGROUPS, 0].astype(jnp.int32)
    padded = (cnt + tms - 1) // tms * tms
    ends = jnp.cumsum(padded)
    starts = ends - padded
    grp = grp.reshape(t)
    pos = rank.reshape(t) + sum(jnp.where(grp == g, starts[g], 0) for g in range(N_EGROUPS))
    t_pad = t + N_EGROUPS * tms
    tile_start = jnp.arange(t_pad // tms, dtype=jnp.int32) * tms
    tile_group = jnp.minimum(sum((tile_start >= ends[g]).astype(jnp.int32) for g in range(N_EGROUPS)),
                             N_EGROUPS - 1)
    tile_valid = (tile_start < ends[N_EGROUPS - 1]).astype(jnp.int32)
    return pos, tile_group, tile_valid, t_pad


def _final_norm_kernel(x_ref, g_ref, o_ref):
    o_ref[...] = _rmsnorm(x_ref[...], g_ref[...])


def _final_norm(x, g, tm):
    t = x.shape[0]
    return pl.pallas_call(
        _final_norm_kernel,
        grid=(t // tm,),
        in_specs=[pl.BlockSpec((tm, D_MODEL), lambda i: (i, 0)), pl.BlockSpec((1, D_MODEL), lambda i: (0, 0))],
        out_specs=pl.BlockSpec((tm, D_MODEL), lambda i: (i, 0)),
        out_shape=jax.ShapeDtypeStruct((t, D_MODEL), F32),
        compiler_params=_cparams("parallel"),
        name="final_norm",
    )(x, g)


def _band_bias(rel_bias):
    rel = (np.arange(3 * BLK)[None, :] - BLK) - np.arange(BLK)[:, None]
    nb = N_BUCKETS // 2
    max_exact = nb // 2
    ret = (rel > 0).astype(np.int32) * nb
    n = np.abs(rel)
    large = max_exact + (np.log(np.maximum(n, 1) / max_exact) / np.log(MAX_DISTANCE / max_exact)
                         * (nb - max_exact)).astype(np.int32)
    large = np.minimum(large, nb - 1)
    bucket = (ret + np.where(n < max_exact, n, large)).astype(np.int32)
    bias = jnp.transpose(rel_bias.astype(F32)[jnp.asarray(bucket)], (2, 0, 1))
    return jnp.where(jnp.asarray(n <= WINDOW)[None], bias, MASKED)


def _head_expand_matrix(first_lane):
    e = np.zeros((LANE, D_INNER), np.float32)
    for h in range(N_SSD_HEADS):
        e[first_lane + h, h * SSD_HEAD_DIM:(h + 1) * SSD_HEAD_DIM] = 1.0
    return jnp.asarray(e, BF16)


def _prep_layers(P):
    depth = P["w_in"].shape[0]
    w = P["w_in"]
    o = np.cumsum([0, Q_DIM, KV_DIM, KV_DIM, D_INNER, CONV_DIM, N_SSD_HEADS, N_SSD_HEADS, D_MODEL, D_MODEL])
    q, k, v, z, xbc, dtf, dtb, ga, gs = [w[:, :, o[i]:o[i + 1]] for i in range(9)]
    dup = lambda a: jnp.concatenate([a[..., :HEAD_DIM]] * 2 + [a[..., HEAD_DIM:]] * 2, axis=-1)
    dt_pad = jnp.zeros((depth, D_MODEL, DT_COLS - 2 * N_SSD_HEADS), F32)
    w_in = jnp.concatenate([q * (HEAD_DIM ** -0.5), dup(k), dup(v), z, xbc, ga, gs, dtf, dtb, dt_pad],
                           axis=-1).astype(BF16)

    pad_lanes = lambda a: jnp.pad(a, ((0, 0), (0, LANE - a.shape[-1])))
    par = jnp.stack([pad_lanes(jnp.concatenate([P["dt_bias_f"], P["dt_bias_b"]], axis=-1)),
                     pad_lanes(jnp.concatenate([P["a_log_f"], P["a_log_b"]], axis=-1))], axis=1)
    par = jnp.pad(par.astype(F32), ((0, 0), (0, 6), (0, 0)))

    wr = jnp.concatenate([P["router_g"], P["router_e"]], axis=-1)
    wr = jnp.pad(wr, ((0, 0), (0, 0), (0, LANE - wr.shape[-1])))
    wr_hi = wr.astype(BF16)
    wr_lo = (wr - wr_hi.astype(F32)).astype(BF16)
    rb = pad_lanes(jnp.concatenate([P["router_g_b"], P["router_e_b"]], axis=-1))[:, None, :]

    row = lambda a: a.astype(F32)[:, None, :]
    return dict(
        g_mix=row(P["g_mix"]), w_in=w_in, sink=P["attn_sink"].astype(F32),
        w_attn_out=P["w_attn_out"].astype(BF16),
        conv_w=P["conv_w"].astype(F32), conv_b=row(P["conv_b"]), par=par,
        d_skip=row(jnp.repeat(P["d_skip"], SSD_HEAD_DIM, axis=-1)), ssd_norm_g=row(P["ssd_norm_g"]),
        w_ssd_out=P["w_ssd_out"].astype(BF16), w_out=P["w_out"].astype(BF16),
        g_ffn=row(P["g_ffn"]), wr_hi=wr_hi, wr_lo=wr_lo, rb=rb.astype(F32),
        w_gate_up=P["w_gate_up"].astype(BF16), w_down=P["w_down"].astype(BF16),
        g_ple=row(P["g_ple"]), w_ple_gate=P["w_ple_gate"].astype(BF16),
        w_ple_proj=P["w_ple_proj"].astype(BF16),
    )


def _tile(n, want):
    t = min(n, want)
    assert n % t == 0
    return t


def _trunk(x, p, layers, bias, e_f, e_b, g_final):
    b, l, _ = x.shape
    t = b * l
    tm = _tile(t, 512)
    seq_tile = _tile(l, 512)
    cps = seq_tile // CHUNK

    def layer(xf, lw):
        W, pl_i = lw
        qkv, z, xbc, gag, dt = _inproj(xf, W["g_mix"], W["w_in"], tm)
        att = _attention(qkv.reshape(b, l, QKV_COLS), W["sink"], bias, seq_tile)
        xs, bc = _conv(xbc.reshape(b, l, CONV_DIM), W["conv_w"], W["conv_b"], seq_tile)
        yf, yb = _ssd(xs, bc, dt.reshape(b, l, DT_COLS), W["par"], e_f, e_b, cps)
        x1, h2g, grp, rank, counts = _merge(
            yf.reshape(t, D_INNER), yb.reshape(t, D_INNER), xs.reshape(t, D_INNER), z, att.reshape(t, Q_DIM), gag,
            xf, W["d_skip"], W["ssd_norm_g"], W["w_ssd_out"], W["w_attn_out"], W["w_out"], W["g_ffn"],
            W["wr_hi"], W["wr_lo"], W["rb"], tm)
        pos, tile_group, tile_valid, t_pad = _route(grp, rank, counts, t, tm)
        hs = _permute(h2g, pos, t_pad, tm)
        ys = _experts(hs, tile_group, tile_valid, W["w_gate_up"], W["w_down"], tm)
        x3 = _ple(x1, ys, pos, pl_i, W["g_ple"], W["w_ple_gate"], W["w_ple_proj"], tm)
        return x3, None

    depth = p.shape[0]
    xf, _ = lax.scan(layer, x.reshape(t, D_MODEL), (layers, p.reshape(depth, t, PLE_DIM)))
    return _final_norm(xf, g_final.astype(F32)[None, :], tm).reshape(b, l, D_MODEL)


def kernel(x_prompt, x_sample, p_prompt, p_sample, g_mix, w_in, rel_bias, attn_sink, w_attn_out, conv_w, conv_b, a_log_f, a_log_b, dt_bias_f, dt_bias_b, d_skip, ssd_norm_g, w_ssd_out, w_out, g_ffn, router_g, router_g_b, router_e, router_e_b, w_gate_up, w_down, g_ple, w_ple_gate, w_ple_proj, g_final):
    P = dict(g_mix=g_mix, w_in=w_in, attn_sink=attn_sink, w_attn_out=w_attn_out, conv_w=conv_w, conv_b=conv_b,
             a_log_f=a_log_f, a_log_b=a_log_b, dt_bias_f=dt_bias_f, dt_bias_b=dt_bias_b, d_skip=d_skip,
             ssd_norm_g=ssd_norm_g, w_ssd_out=w_ssd_out, w_out=w_out, g_ffn=g_ffn, router_g=router_g,
             router_g_b=router_g_b, router_e=router_e, router_e_b=router_e_b, w_gate_up=w_gate_up,
             w_down=w_down, g_ple=g_ple, w_ple_gate=w_ple_gate, w_ple_proj=w_ple_proj)
    layers = _prep_layers(P)
    bias = _band_bias(rel_bias)
    e_f = _head_expand_matrix(0)
    e_b = _head_expand_matrix(N_SSD_HEADS)
    y_prompt = _trunk(x_prompt, p_prompt, layers, bias, e_f, e_b, g_final)
    y_sample = _trunk(x_sample, p_sample, layers, bias, e_f, e_b, g_final)
    return (y_prompt, y_sample)
```

```python
import functools

import numpy as np
import jax
import jax.numpy as jnp
from jax import lax
from jax.experimental import pallas as pl
from jax.experimental.pallas import tpu as pltpu

F32 = jnp.float32
BF16 = jnp.bfloat16

LANE = 128
HALF_LANE = LANE // 2
VMEM_LIMIT_BYTES = 56 * 1024 * 1024

D_MODEL = 1024
HEAD_DIM = 64
N_HEADS = 8
N_KV_HEADS = 2
GQ = N_HEADS // N_KV_HEADS
Q_DIM = N_HEADS * HEAD_DIM
KV_DIM = N_KV_HEADS * HEAD_DIM
WINDOW = 128
BLK = 128
N_BUCKETS = 32
MAX_DISTANCE = 128
D_INNER = 1024
SSD_HEAD_DIM = 64
N_SSD_HEADS = D_INNER // SSD_HEAD_DIM
N_SSD_GROUPS = 2
D_STATE = 64
CHUNK = 128
CONV_WIDTH = 5
CONV_PAD = CONV_WIDTH // 2
BC_DIM = N_SSD_GROUPS * D_STATE
CONV_DIM = D_INNER + 2 * BC_DIM
N_EGROUPS = 4
EXPERTS_PER_GROUP = 4
N_EXPERTS = N_EGROUPS * EXPERTS_PER_GROUP
D_EXPERT = 256
PLE_DIM = 256
EPS = 1e-6
MASKED = -1e30

QKV_COLS = Q_DIM + 2 * KV_DIM
GATE_COLS = 2 * D_MODEL
DT_COLS = LANE
IN_COLS_PADDED = QKV_COLS + D_INNER + CONV_DIM + GATE_COLS + DT_COLS
HALO = 16
ROUTED_WORDS = D_MODEL // 2


def _cparams(*sem):
    return pltpu.CompilerParams(dimension_semantics=sem, vmem_limit_bytes=VMEM_LIMIT_BYTES)


def _dot(a, b):
    return jnp.dot(a, b, preferred_element_type=F32)


def _dot_nt(a, b):
    return lax.dot_general(a, b, (((1,), (1,)), ((), ())), preferred_element_type=F32)


def _split_bf16(v, terms):
    parts = []
    r = v
    for _ in range(terms):
        p = r.astype(BF16)
        parts.append(p)
        r = r - p.astype(F32)
    return parts


def _rmsnorm(x, g):
    return x * lax.rsqrt(jnp.mean(x * x, axis=-1, keepdims=True) + EPS) * g


def _sigmoid(x):
    return 1.0 / (1.0 + jnp.exp(-x))


def _inproj_kernel(x_ref, g_ref, w_ref, qkv_ref, z_ref, xbc_ref, gag_ref, dt_ref):
    hb = _rmsnorm(x_ref[...], g_ref[...]).astype(BF16)
    off = 0
    for ref in (qkv_ref, z_ref, xbc_ref, gag_ref, dt_ref):
        n = ref.shape[-1]
        ref[...] = _dot(hb, w_ref[:, off:off + n]).astype(ref.dtype)
        off += n


def _inproj(x, g, w, tm):
    t = x.shape[0]
    widths = (QKV_COLS, D_INNER, CONV_DIM, GATE_COLS, DT_COLS)
    dtypes = (BF16, BF16, BF16, BF16, F32)
    return pl.pallas_call(
        _inproj_kernel,
        grid=(t // tm,),
        in_specs=[
            pl.BlockSpec((tm, D_MODEL), lambda i: (i, 0)),
            pl.BlockSpec((1, D_MODEL), lambda i: (0, 0)),
            pl.BlockSpec((D_MODEL, IN_COLS_PADDED), lambda i: (0, 0)),
        ],
        out_specs=[pl.BlockSpec((tm, n), lambda i: (i, 0)) for n in widths],
        out_shape=[jax.ShapeDtypeStruct((t, n), d) for n, d in zip(widths, dtypes)],
        compiler_params=_cparams("parallel"),
        name="inproj",
    )(x, g, w)


STACK_HEAD_ORDER = tuple(half * GQ + p for p in range(GQ) for half in range(N_KV_HEADS))


SUBS_PER_TRIP = 2


def _attn_kernel(q_ref, kp_ref, km_ref, kn_ref, vp_ref, vm_ref, vn_ref, bias_ref, sink_ref, o_ref,
                 k_ext, v_ext, *, qb, nq):
    n = pl.program_id(1)
    k_ext[0:BLK] = kp_ref[0]
    k_ext[BLK:BLK + qb] = km_ref[0]
    k_ext[BLK + qb:] = kn_ref[0]
    v_ext[0:BLK, 0:KV_DIM] = vp_ref[0]
    v_ext[BLK:BLK + qb, 0:KV_DIM] = vm_ref[0]
    v_ext[BLK + qb:, 0:KV_DIM] = vn_ref[0]
    v_ext[:, KV_DIM:] = jnp.ones((qb + 2 * BLK, LANE), v_ext.dtype)
    low = lax.broadcasted_iota(jnp.int32, (BLK, LANE), 1) < HALF_LANE
    last_sub = qb // BLK - 1

    def sub_block(j):
        r0 = pl.multiple_of(j * BLK, BLK)
        q = q_ref[0, pl.ds(r0, BLK), :]
        zero = jnp.zeros((BLK, LANE), q.dtype)
        parts = []
        for p in range(GQ):
            qp = q[:, p * LANE:(p + 1) * LANE]
            parts += [jnp.where(low, qp, zero), jnp.where(low, zero, qp)]
        qs = jnp.concatenate(parts, axis=0)
        first = jnp.logical_and(n == 0, j == 0).astype(jnp.int32)
        last = jnp.logical_and(n == nq - 1, j == last_sub).astype(jnp.int32)
        s = _dot_nt(qs, k_ext[pl.ds(r0, 3 * BLK), :]) + bias_ref[first + 2 * last]
        sc = [s[:, c * BLK:(c + 1) * BLK] for c in range(3)]
        sk = sink_ref[...]
        row_max = jnp.max(jnp.maximum(jnp.maximum(sc[0], sc[1]), sc[2]), axis=-1, keepdims=True)
        m = jnp.maximum(row_max, sk)
        p_un = jnp.concatenate([jnp.exp(c - m) for c in sc], axis=1).astype(BF16)
        ov = _dot(p_un, v_ext[pl.ds(r0, 3 * BLK), :])
        o = ov[:, 0:KV_DIM] / (ov[:, KV_DIM:] + jnp.exp(sk - m))
        for p in range(GQ):
            o_lo = o[(2 * p) * BLK:(2 * p + 1) * BLK]
            o_hi = o[(2 * p + 1) * BLK:(2 * p + 2) * BLK]
            o_ref[0, pl.ds(r0, BLK), p * LANE:(p + 1) * LANE] = jnp.where(low, o_lo, o_hi).astype(o_ref.dtype)

    def trip(jj, carry):
        for u in range(SUBS_PER_TRIP):
            sub_block(jj * SUBS_PER_TRIP + u)
        return carry

    lax.fori_loop(0, qb // (BLK * SUBS_PER_TRIP), trip, 0)


def _attention(qkv, sink_rows, bias, qb):
    b, l, _ = qkv.shape
    nq = l // qb
    r = qb // BLK
    nblk = l // BLK
    kcol, vcol = Q_DIM // LANE, Q_DIM // LANE + 1

    def prev_map(c):
        return lambda bi, n: (bi, jnp.maximum(n * r - 1, 0), c)

    def next_map(c):
        return lambda bi, n: (bi, jnp.minimum((n + 1) * r, nblk - 1), c)

    return pl.pallas_call(
        functools.partial(_attn_kernel, qb=qb, nq=nq),
        grid=(b, nq),
        in_specs=[
            pl.BlockSpec((1, qb, Q_DIM), lambda bi, n: (bi, n, 0)),
            pl.BlockSpec((1, BLK, KV_DIM), prev_map(kcol)),
            pl.BlockSpec((1, qb, KV_DIM), lambda bi, n: (bi, n, kcol)),
            pl.BlockSpec((1, BLK, KV_DIM), next_map(kcol)),
            pl.BlockSpec((1, BLK, KV_DIM), prev_map(vcol)),
            pl.BlockSpec((1, qb, KV_DIM), lambda bi, n: (bi, n, vcol)),
            pl.BlockSpec((1, BLK, KV_DIM), next_map(vcol)),
            pl.BlockSpec((4, N_HEADS * BLK, 3 * BLK), lambda bi, n: (0, 0, 0)),
            pl.BlockSpec((N_HEADS * BLK, LANE), lambda bi, n: (0, 0)),
        ],
        out_specs=pl.BlockSpec((1, qb, Q_DIM), lambda bi, n: (bi, n, 0)),
        out_shape=jax.ShapeDtypeStruct((b, l, Q_DIM), BF16),
        scratch_shapes=[pltpu.VMEM((qb + 2 * BLK, KV_DIM), BF16),
                        pltpu.VMEM((qb + 2 * BLK, KV_DIM + LANE), BF16)],
        compiler_params=_cparams("parallel", "parallel"),
        name="attention",
    )(qkv, qkv, qkv, qkv, qkv, qkv, qkv, bias, sink_rows)


def _conv_kernel(p_ref, m_ref, n_ref, w_ref, b_ref, xs_ref, bc_ref, ext, *, tm, nblk):
    i = pl.program_id(1)
    ext[0:HALO] = jnp.where(i > 0, p_ref[0].astype(F32), 0.0)
    ext[HALO:HALO + tm] = m_ref[0].astype(F32)
    ext[HALO + tm:] = jnp.where(i < nblk - 1, n_ref[0].astype(F32), 0.0)
    for cb in range(CONV_DIM // LANE):
        cs = slice(cb * LANE, (cb + 1) * LANE)
        acc = b_ref[:, cs] + w_ref[0:1, cs] * ext[HALO - CONV_PAD:HALO - CONV_PAD + tm, cs]
        for k in range(1, CONV_WIDTH):
            acc = acc + w_ref[k:k + 1, cs] * ext[HALO - CONV_PAD + k:HALO - CONV_PAD + k + tm, cs]
        y = (acc * _sigmoid(acc))
        if cb < D_INNER // LANE:
            xs_ref[0, :, cs] = y.astype(xs_ref.dtype)
        else:
            c0 = cb * LANE - D_INNER
            bc_ref[0, :, c0:c0 + LANE] = y.astype(bc_ref.dtype)


def _conv(xbc, w, bvec, tm):
    b, l, _ = xbc.shape
    nblk = l // tm
    r = tm // HALO
    nh = l // HALO
    return pl.pallas_call(
        functools.partial(_conv_kernel, tm=tm, nblk=nblk),
        grid=(b, nblk),
        in_specs=[
            pl.BlockSpec((1, HALO, CONV_DIM), lambda bi, i: (bi, jnp.maximum(i * r - 1, 0), 0)),
            pl.BlockSpec((1, tm, CONV_DIM), lambda bi, i: (bi, i, 0)),
            pl.BlockSpec((1, HALO, CONV_DIM), lambda bi, i: (bi, jnp.minimum((i + 1) * r, nh - 1), 0)),
            pl.BlockSpec((CONV_WIDTH, CONV_DIM), lambda bi, i: (0, 0)),
            pl.BlockSpec((1, CONV_DIM), lambda bi, i: (0, 0)),
        ],
        out_specs=[
            pl.BlockSpec((1, tm, D_INNER), lambda bi, i: (bi, i, 0)),
            pl.BlockSpec((1, tm, 2 * BC_DIM), lambda bi, i: (bi, i, 0)),
        ],
        out_shape=[jax.ShapeDtypeStruct((b, l, D_INNER), BF16),
                   jax.ShapeDtypeStruct((b, l, 2 * BC_DIM), BF16)],
        scratch_shapes=[pltpu.VMEM((tm + 2 * HALO, CONV_DIM), F32)],
        compiler_params=_cparams("parallel", "parallel"),
        name="conv_silu",
    )(xbc, xbc, xbc, w, bvec)


def _ssd_chunk(x_ref, bc_ref, dt_ref, par_ref, e_ref, y_ref, s_ref, r0, *, fwd):
    rows = lax.broadcasted_iota(jnp.int32, (CHUNK, CHUNK), 0)
    cols = lax.broadcasted_iota(jnp.int32, (CHUNK, CHUNK), 1)
    mask = rows >= cols if fwd else rows <= cols
    low = cols < HALF_LANE
    lane0 = 0 if fwd else N_SSD_HEADS

    x = dt_ref[0, pl.ds(r0, CHUNK), :] + par_ref[0:1, :]
    dt = jnp.maximum(x, 0.0) + jnp.log(1.0 + jnp.exp(-jnp.abs(x)))
    a = dt * (-jnp.exp(par_ref[1:2, :]))
    tri = mask.astype(BF16)
    acs = sum(_dot(tri, part) for part in _split_bf16(a, 3))
    last = acs[CHUNK - 1:CHUNK, :] if fwd else acs[0:1, :]
    acs_t = acs.T
    dt_t = dt.T

    bc = bc_ref[0, pl.ds(r0, CHUNK), :]
    b_all = bc[:, 0:BC_DIM]
    c_all = bc[:, BC_DIM:2 * BC_DIM]
    b_t = b_all.astype(F32).T.astype(BF16)
    zero = jnp.zeros_like(c_all)
    c_grp = [jnp.where(low, c_all, zero), jnp.where(low, zero, c_all)]
    cb = [_dot_nt(c, b_all) for c in c_grp]

    def expand(v, terms):
        return sum(_dot(part, e_ref[...]) for part in _split_bf16(v, terms))

    w_state = expand(dt * jnp.exp(last - acs), 1)
    w_off = expand(jnp.exp(acs), 1)
    decay = expand(jnp.broadcast_to(jnp.exp(last), (8, LANE)), 2)[0:1, :]

    xc = x_ref[0, pl.ds(r0, CHUNK), :]
    xw = (xc.astype(F32) * w_state).astype(BF16)
    for p in range(N_SSD_HEADS // 2):
        g = p // (N_SSD_HEADS // (2 * N_SSD_GROUPS))
        ps = slice(p * LANE, (p + 1) * LANE)
        xp = xc[:, ps]
        ys = []
        for half in range(2):
            lh = lane0 + 2 * p + half
            seg = acs[:, lh:lh + 1] - acs_t[lh:lh + 1, :]
            decay_mat = jnp.exp(jnp.where(mask, seg, MASKED))
            m = (cb[g] * decay_mat * dt_t[lh:lh + 1, :]).astype(BF16)
            ys.append(_dot(m, xp))
        state = s_ref[p]
        y_off = _dot(c_grp[g], state.astype(BF16)) * w_off[:, ps]
        y_ref[0, pl.ds(r0, CHUNK), ps] = (jnp.where(low, ys[0], ys[1]) + y_off).astype(y_ref.dtype)
        s_ref[p] = state * decay[:, ps] + _dot(b_t, xw[:, ps])


def _ssd_kernel(xf_ref, bcf_ref, dtf_ref, xb_ref, bcb_ref, dtb_ref, par_ref, ef_ref, eb_ref,
                yf_ref, yb_ref, sf_ref, sb_ref, *, cps):
    @pl.when(pl.program_id(1) == 0)
    def _():
        sf_ref[...] = jnp.zeros_like(sf_ref)
        sb_ref[...] = jnp.zeros_like(sb_ref)

    def body(j, carry):
        rf = pl.multiple_of(j * CHUNK, CHUNK)
        rb = pl.multiple_of((cps - 1 - j) * CHUNK, CHUNK)
        _ssd_chunk(xf_ref, bcf_ref, dtf_ref, par_ref, ef_ref, yf_ref, sf_ref, rf, fwd=True)
        _ssd_chunk(xb_ref, bcb_ref, dtb_ref, par_ref, eb_ref, yb_ref, sb_ref, rb, fwd=False)
        return carry

    lax.fori_loop(0, cps, body, 0)


def _ssd(xs, bc, dt, par, e_f, e_b, cps):
    b, l, _ = xs.shape
    rows = cps * CHUNK
    ns = l // rows
    fmap = lambda bi, c: (bi, c, 0)
    bmap = lambda bi, c: (bi, ns - 1 - c, 0)
    const = lambda bi, c: (0, 0)
    state = pltpu.VMEM((N_SSD_HEADS // 2, LANE, LANE), F32)
    return pl.pallas_call(
        functools.partial(_ssd_kernel, cps=cps),
        grid=(b, ns),
        in_specs=[
            pl.BlockSpec((1, rows, D_INNER), fmap),
            pl.BlockSpec((1, rows, 2 * BC_DIM), fmap),
            pl.BlockSpec((1, rows, DT_COLS), fmap),
            pl.BlockSpec((1, rows, D_INNER), bmap),
            pl.BlockSpec((1, rows, 2 * BC_DIM), bmap),
            pl.BlockSpec((1, rows, DT_COLS), bmap),
            pl.BlockSpec((8, LANE), const),
            pl.BlockSpec((LANE, D_INNER), const),
            pl.BlockSpec((LANE, D_INNER), const),
        ],
        out_specs=[pl.BlockSpec((1, rows, D_INNER), fmap), pl.BlockSpec((1, rows, D_INNER), bmap)],
        out_shape=[jax.ShapeDtypeStruct((b, l, D_INNER), BF16)] * 2,
        scratch_shapes=[state, state],
        compiler_params=_cparams("parallel", "arbitrary"),
        name="ssd_scan",
    )(xs, bc, dt, xs, bc, dt, par, e_f, e_b)


def _router_gates(lg_t):
    row = lambda j: lg_t[j:j + 1, :]
    lgrp = [row(j) for j in range(N_EGROUPS)]
    m = functools.reduce(jnp.maximum, lgrp)
    g_sel = jnp.where(lgrp[0] == m, 0, jnp.where(lgrp[1] == m, 1, jnp.where(lgrp[2] == m, 2, 3)))
    p_sel = 1.0 / sum(jnp.exp(v - m) for v in lgrp)
    s = []
    for k in range(EXPERTS_PER_GROUP):
        e = [row(N_EGROUPS + EXPERTS_PER_GROUP * g + k) for g in range(N_EGROUPS)]
        s.append(jnp.where(g_sel == 0, e[0], jnp.where(g_sel == 1, e[1], jnp.where(g_sel == 2, e[2], e[3]))))
    m1 = functools.reduce(jnp.maximum, s)
    i1 = jnp.where(s[0] == m1, 0, jnp.where(s[1] == m1, 1, jnp.where(s[2] == m1, 2, 3)))
    s2 = [jnp.where(i1 == k, -jnp.inf, s[k]) for k in range(EXPERTS_PER_GROUP)]
    m2 = functools.reduce(jnp.maximum, s2)
    i2 = jnp.where(s2[0] == m2, 0, jnp.where(s2[1] == m2, 1, jnp.where(s2[2] == m2, 2, 3)))
    t = jnp.exp(m2 - m1)
    w1 = 1.0 / (1.0 + t)
    w2 = t / (1.0 + t)
    tm = lg_t.shape[1]
    rows = lax.broadcasted_iota(jnp.int32, (N_EXPERTS, tm), 0)
    gates = jnp.zeros((N_EXPERTS, tm), F32)
    for g in range(N_EGROUPS):
        for k in range(EXPERTS_PER_GROUP):
            wk = jnp.where(i1 == k, w1, jnp.where(i2 == k, w2, 0.0))
            val = jnp.where(g_sel == g, wk * p_sel, 0.0)
            gates = jnp.where(rows == g * EXPERTS_PER_GROUP + k, val, gates)
    return jnp.concatenate([gates, jnp.zeros((LANE - N_EXPERTS, tm), F32)], axis=0), g_sel


def _pack_bf16_pairs(h):
    bits = lax.bitcast_convert_type(h.astype(BF16).astype(F32), jnp.uint32)
    half = h.shape[1] // 2
    return (bits[:, :half] >> 16) | (bits[:, half:] & jnp.uint32(0xFFFF0000))


def _unpack_bf16_pairs(w):
    lo = lax.bitcast_convert_type(w << 16, F32).astype(BF16)
    hi = lax.bitcast_convert_type(w & jnp.uint32(0xFFFF0000), F32).astype(BF16)
    return lo, hi


def _merge_kernel(yf_ref, yb_ref, xs_ref, z_ref, att_ref, gag_ref, x_ref, dsk_ref, ng_ref,
                  wso_ref, wao_ref, wo_ref, gf_ref, wrh_ref, wrl_ref, rb_ref, upper_ref,
                  x1_ref, h2g_ref, grp_ref, rank_ref, counts_ref, cnt):
    @pl.when(pl.program_id(0) == 0)
    def _():
        cnt[...] = jnp.zeros_like(cnt)

    y = yf_ref[...].astype(F32) + yb_ref[...].astype(F32) + dsk_ref[...] * xs_ref[...].astype(F32)
    z = z_ref[...].astype(F32)
    y = y * (z * _sigmoid(z))
    gsz = D_INNER // N_SSD_GROUPS
    parts = []
    for g in range(N_SSD_GROUPS):
        yg = y[:, g * gsz:(g + 1) * gsz]
        parts.append(yg * lax.rsqrt(jnp.mean(yg * yg, axis=-1, keepdims=True) + EPS))
    ysn = (jnp.concatenate(parts, axis=1) * ng_ref[...]).astype(BF16)
    ys = _dot(ysn, wso_ref[...])
    ya = _dot(att_ref[...], wao_ref[...])
    ga = gag_ref[:, 0:D_MODEL].astype(F32)
    gs = gag_ref[:, D_MODEL:2 * D_MODEL].astype(F32)
    mixed = _sigmoid(ga) * ya + _sigmoid(gs) * ys
    x1 = x_ref[...] + _dot(mixed.astype(BF16), wo_ref[...])
    x1_ref[...] = x1
    h2 = _rmsnorm(x1, gf_ref[...])
    hh, hl = _split_bf16(h2, 2)
    lg = _dot(hh, wrh_ref[...]) + _dot(hl, wrh_ref[...]) + _dot(hh, wrl_ref[...]) + rb_ref[...]
    gates_t, g_sel = _router_gates(lg.T)
    h2g_ref[:, 0:ROUTED_WORDS] = _pack_bf16_pairs(h2)
    h2g_ref[:, ROUTED_WORDS:] = lax.bitcast_convert_type(gates_t.T, jnp.uint32)
    tm = x1.shape[0]
    onehot = (lax.broadcasted_iota(jnp.int32, (8, tm), 0) == g_sel).astype(F32)
    rank_all = _dot(onehot.astype(BF16), upper_ref[...]) + cnt[:, 0:1]
    rank_ref[0] = jnp.sum(onehot * rank_all, axis=0, keepdims=True).astype(jnp.int32)
    grp_ref[0] = g_sel
    cnt[...] += jnp.sum(onehot, axis=1, keepdims=True)
    counts_ref[...] = cnt[...]


def _merge(yf, yb, xs, z, att, gag, x, dsk, ng, wso, wao, wo, gf, wrh, wrl, rb, tm):
    t = x.shape[0]
    nt = t // tm
    upper = jnp.asarray(np.triu(np.ones((tm, tm), np.float32), 1), BF16)
    tile = lambda n: pl.BlockSpec((tm, n), lambda i: (i, 0))
    full = lambda a: pl.BlockSpec(a.shape, lambda i: (0,) * a.ndim)
    row = pl.BlockSpec((1, 1, tm), lambda i: (i, 0, 0))
    consts = (dsk, ng, wso, wao, wo, gf, wrh, wrl, rb, upper)
    return pl.pallas_call(
        _merge_kernel,
        grid=(nt,),
        in_specs=[tile(D_INNER), tile(D_INNER), tile(D_INNER), tile(D_INNER), tile(Q_DIM),
                  tile(GATE_COLS), tile(D_MODEL)] + [full(a) for a in consts],
        out_specs=[tile(D_MODEL), tile(ROUTED_WORDS + LANE), row, row, pl.BlockSpec((8, LANE), lambda i: (0, 0))],
        out_shape=[jax.ShapeDtypeStruct((t, D_MODEL), F32),
                   jax.ShapeDtypeStruct((t, ROUTED_WORDS + LANE), jnp.uint32),
                   jax.ShapeDtypeStruct((nt, 1, tm), jnp.int32),
                   jax.ShapeDtypeStruct((nt, 1, tm), jnp.int32),
                   jax.ShapeDtypeStruct((8, LANE), F32)],
        scratch_shapes=[pltpu.VMEM((8, LANE), F32)],
        compiler_params=_cparams("arbitrary"),
        name="merge_router",
    )(yf, yb, xs, z, att, gag, x, *consts)


def _dma_params(*sem):
    return pltpu.CompilerParams(dimension_semantics=sem, vmem_limit_bytes=VMEM_LIMIT_BYTES,
                                disable_bounds_checks=True)


def _permute_kernel(pos_ref, src_ref, init_ref, dst_ref, sem, *, rows):
    del init_ref

    def issue(r, carry):
        pltpu.make_async_copy(src_ref.at[pl.ds(r, 1)], dst_ref.at[pl.ds(pos_ref[0, 0, r], 1)], sem).start()
        return carry

    lax.fori_loop(0, rows, issue, 0, unroll=8)
    pltpu.make_async_copy(src_ref, dst_ref.at[pl.ds(0, rows)], sem).wait()


def _permute(h2g, pos, t_pad, rows):
    t, w = h2g.shape
    nt = t // rows
    return pl.pallas_call(
        functools.partial(_permute_kernel, rows=rows),
        grid=(nt,),
        in_specs=[pl.BlockSpec((1, 1, rows), lambda i: (i, 0, 0), memory_space=pltpu.SMEM),
                  pl.BlockSpec((rows, w), lambda i: (i, 0)),
                  pl.BlockSpec(memory_space=pl.ANY)],
        out_specs=pl.BlockSpec(memory_space=pl.ANY),
        out_shape=jax.ShapeDtypeStruct((t_pad, w), h2g.dtype),
        scratch_shapes=[pltpu.SemaphoreType.DMA(())],
        input_output_aliases={2: 0},
        compiler_params=_dma_params("arbitrary"),
        name="moe_permute",
    )(pos.reshape(nt, 1, rows), h2g, jnp.zeros((t_pad, w), h2g.dtype))


def _experts_kernel(tg_ref, tv_ref, hs_ref, wgu_ref, wd_ref, o_ref):
    i = pl.program_id(0)

    @pl.when(tv_ref[i] == 0)
    def _():
        o_ref[...] = jnp.zeros_like(o_ref)

    @pl.when(tv_ref[i] != 0)
    def _():
        lo, hi = _unpack_bf16_pairs(hs_ref[:, 0:ROUTED_WORDS])
        gates = lax.bitcast_convert_type(hs_ref[:, ROUTED_WORDS:], F32)
        lanes = lax.broadcasted_iota(jnp.int32, gates.shape, 1)
        first = tg_ref[i] * EXPERTS_PER_GROUP
        acc = jnp.zeros(o_ref.shape, F32)
        for k in range(EXPERTS_PER_GROUP):
            gcol = jnp.sum(jnp.where(lanes == first + k, gates, 0.0), axis=1, keepdims=True)
            gu = _dot(lo, wgu_ref[0, k, 0:ROUTED_WORDS, :]) + _dot(hi, wgu_ref[0, k, ROUTED_WORDS:, :])
            g = gu[:, 0:D_EXPERT]
            u = gu[:, D_EXPERT:2 * D_EXPERT]
            a = (g * _sigmoid(g)) * u * gcol
            acc = acc + _dot(a.astype(BF16), wd_ref[0, k])
        o_ref[...] = acc


def _experts(hs, tile_group, tile_valid, wgu, wd, tms):
    t_pad, w = hs.shape
    wgu = wgu.reshape(N_EGROUPS, EXPERTS_PER_GROUP, D_MODEL, 2 * D_EXPERT)
    wd = wd.reshape(N_EGROUPS, EXPERTS_PER_GROUP, D_EXPERT, D_MODEL)
    return pl.pallas_call(
        _experts_kernel,
        grid_spec=pltpu.PrefetchScalarGridSpec(
            num_scalar_prefetch=2,
            grid=(t_pad // tms,),
            in_specs=[
                pl.BlockSpec((tms, w), lambda i, tg, tv: (i, 0)),
                pl.BlockSpec((1, EXPERTS_PER_GROUP, D_MODEL, 2 * D_EXPERT), lambda i, tg, tv: (tg[i], 0, 0, 0)),
                pl.BlockSpec((1, EXPERTS_PER_GROUP, D_EXPERT, D_MODEL), lambda i, tg, tv: (tg[i], 0, 0, 0)),
            ],
            out_specs=pl.BlockSpec((tms, D_MODEL), lambda i, tg, tv: (i, 0)),
        ),
        out_shape=jax.ShapeDtypeStruct((t_pad, D_MODEL), F32),
        compiler_params=_cparams("arbitrary"),
        name="moe_experts",
    )(tile_group, tile_valid, hs, wgu, wd)


def _ple_kernel(pos_ref, posn_ref, x_ref, p_ref, g_ref, wg_ref, wp_ref, ys_ref, o_ref, buf, sem, *, tm, nt):
    i = pl.program_id(0)
    slot = i % 2

    def gather(pref, s):
        def issue(r, carry):
            pltpu.make_async_copy(ys_ref.at[pl.ds(pref[0, 0, r], 1)], buf.at[s, pl.ds(r, 1)], sem.at[s]).start()
            return carry
        lax.fori_loop(0, tm, issue, 0, unroll=8)

    @pl.when(i == 0)
    def _():
        gather(pos_ref, 0)

    @pl.when(i + 1 < nt)
    def _():
        gather(posn_ref, 1 - slot)

    pltpu.make_async_copy(ys_ref.at[pl.ds(0, tm)], buf.at[slot], sem.at[slot]).wait()
    x = x_ref[...] + buf[slot]
    gate = _sigmoid(_dot(_rmsnorm(x, g_ref[...]).astype(BF16), wg_ref[...]))
    o_ref[...] = x + gate * _dot(p_ref[...].astype(BF16), wp_ref[...])


def _ple(x, ys, pos, p, g, wg, wp, tm):
    t = x.shape[0]
    nt = t // tm
    pos3 = pos.reshape(nt, 1, tm)
    return pl.pallas_call(
        functools.partial(_ple_kernel, tm=tm, nt=nt),
        grid=(nt,),
        in_specs=[
            pl.BlockSpec((1, 1, tm), lambda i: (i, 0, 0), memory_space=pltpu.SMEM),
            pl.BlockSpec((1, 1, tm), lambda i: (jnp.minimum(i + 1, nt - 1), 0, 0), memory_space=pltpu.SMEM),
            pl.BlockSpec((tm, D_MODEL), lambda i: (i, 0)),
            pl.BlockSpec((tm, PLE_DIM), lambda i: (i, 0)),
            pl.BlockSpec((1, D_MODEL), lambda i: (0, 0)),
            pl.BlockSpec((D_MODEL, D_MODEL), lambda i: (0, 0)),
            pl.BlockSpec((PLE_DIM, D_MODEL), lambda i: (0, 0)),
            pl.BlockSpec(memory_space=pl.ANY),
        ],
        out_specs=pl.BlockSpec((tm, D_MODEL), lambda i: (i, 0)),
        out_shape=jax.ShapeDtypeStruct((t, D_MODEL), F32),
        scratch_shapes=[pltpu.VMEM((2, tm, D_MODEL), F32), pltpu.SemaphoreType.DMA((2,))],
        compiler_params=_dma_params("arbitrary"),
        name="ple_gate",
    )(pos3, pos3, x, p, g, wg, wp, ys)


def _route(grp, rank, counts, t, tms):
    cnt = counts[:N_EGROUPS, 0].astype(jnp.int32)
    padded = (cnt + tms - 1) // tms * tms
    ends = jnp.cumsum(padded)
    starts = ends - padded
    grp = grp.reshape(t)
    pos = rank.reshape(t) + sum(jnp.where(grp == g, starts[g], 0) for g in range(N_EGROUPS))
    t_pad = t + N_EGROUPS * tms
    tile_start = jnp.arange(t_pad // tms, dtype=jnp.int32) * tms
    tile_group = jnp.minimum(sum((tile_start >= ends[g]).astype(jnp.int32) for g in range(N_EGROUPS)),
                             N_EGROUPS - 1)
    tile_valid = (tile_start < ends[N_EGROUPS - 1]).astype(jnp.int32)
    return pos, tile_group, tile_valid, t_pad


def _final_norm_kernel(x_ref, g_ref, o_ref):
    o_ref[...] = _rmsnorm(x_ref[...], g_ref[...])


def _final_norm(x, g, tm):
    t = x.shape[0]
    return pl.pallas_call(
        _final_norm_kernel,
        grid=(t // tm,),
        in_specs=[pl.BlockSpec((tm, D_MODEL), lambda i: (i, 0)), pl.BlockSpec((1, D_MODEL), lambda i: (0, 0))],
        out_specs=pl.BlockSpec((tm, D_MODEL), lambda i: (i, 0)),
        out_shape=jax.ShapeDtypeStruct((t, D_MODEL), F32),
        compiler_params=_cparams("parallel"),
        name="final_norm",
    )(x, g)


def _band_bias(rel_bias):
    rel = (np.arange(3 * BLK)[None, :] - BLK) - np.arange(BLK)[:, None]
    nb = N_BUCKETS // 2
    max_exact = nb // 2
    ret = (rel > 0).astype(np.int32) * nb
    n = np.abs(rel)
    large = max_exact + (np.log(np.maximum(n, 1) / max_exact) / np.log(MAX_DISTANCE / max_exact)
                         * (nb - max_exact)).astype(np.int32)
    large = np.minimum(large, nb - 1)
    bucket = (ret + np.where(n < max_exact, n, large)).astype(np.int32)
    bias = jnp.transpose(rel_bias.astype(F32)[jnp.asarray(bucket)], (2, 0, 1))
    bias = jnp.where(jnp.asarray(n <= WINDOW)[None], bias, MASKED)
    bias = bias[np.asarray(STACK_HEAD_ORDER)].reshape(N_HEADS * BLK, 3 * BLK)
    key_blk = jnp.arange(3 * BLK) // BLK
    no_left = jnp.where(key_blk == 0, MASKED, bias)
    no_right = jnp.where(key_blk == 2, MASKED, bias)
    return jnp.stack([bias, no_left, no_right, jnp.where(key_blk == 2, MASKED, no_left)])


def _by_head(a, axis):
    idx = np.concatenate([np.arange(h * HEAD_DIM, (h + 1) * HEAD_DIM) for h in STACK_HEAD_ORDER])
    return jnp.take(a, jnp.asarray(idx), axis=axis)


def _head_expand_matrix(first_lane):
    e = np.zeros((LANE, D_INNER), np.float32)
    for h in range(N_SSD_HEADS):
        e[first_lane + h, h * SSD_HEAD_DIM:(h + 1) * SSD_HEAD_DIM] = 1.0
    return jnp.asarray(e, BF16)


def _prep_layers(P):
    depth = P["w_in"].shape[0]
    w = P["w_in"]
    o = np.cumsum([0, Q_DIM, KV_DIM, KV_DIM, D_INNER, CONV_DIM, N_SSD_HEADS, N_SSD_HEADS, D_MODEL, D_MODEL])
    q, k, v, z, xbc, dtf, dtb, ga, gs = [w[:, :, o[i]:o[i + 1]] for i in range(9)]
    dt_pad = jnp.zeros((depth, D_MODEL, DT_COLS - 2 * N_SSD_HEADS), F32)
    w_in = jnp.concatenate([_by_head(q, 2) * (HEAD_DIM ** -0.5), k, v, z, xbc, ga, gs, dtf, dtb, dt_pad],
                           axis=-1).astype(BF16)

    pad_lanes = lambda a: jnp.pad(a, ((0, 0), (0, LANE - a.shape[-1])))
    par = jnp.stack([pad_lanes(jnp.concatenate([P["dt_bias_f"], P["dt_bias_b"]], axis=-1)),
                     pad_lanes(jnp.concatenate([P["a_log_f"], P["a_log_b"]], axis=-1))], axis=1)
    par = jnp.pad(par.astype(F32), ((0, 0), (0, 6), (0, 0)))

    wr = jnp.concatenate([P["router_g"], P["router_e"]], axis=-1)
    wr = jnp.pad(wr, ((0, 0), (0, 0), (0, LANE - wr.shape[-1])))
    wr_hi = wr.astype(BF16)
    wr_lo = (wr - wr_hi.astype(F32)).astype(BF16)
    rb = pad_lanes(jnp.concatenate([P["router_g_b"], P["router_e_b"]], axis=-1))[:, None, :]

    row = lambda a: a.astype(F32)[:, None, :]
    return dict(
        g_mix=row(P["g_mix"]), w_in=w_in, sink=jnp.broadcast_to(jnp.repeat(P["attn_sink"].astype(F32)[:, np.asarray(STACK_HEAD_ORDER)], BLK, axis=1)[:, :, None],
                              (depth, N_HEADS * BLK, LANE)),
        w_attn_out=_by_head(P["w_attn_out"], 1).astype(BF16),
        conv_w=P["conv_w"].astype(F32), conv_b=row(P["conv_b"]), par=par,
        d_skip=row(jnp.repeat(P["d_skip"], SSD_HEAD_DIM, axis=-1)), ssd_norm_g=row(P["ssd_norm_g"]),
        w_ssd_out=P["w_ssd_out"].astype(BF16), w_out=P["w_out"].astype(BF16),
        g_ffn=row(P["g_ffn"]), wr_hi=wr_hi, wr_lo=wr_lo, rb=rb.astype(F32),
        w_gate_up=P["w_gate_up"].astype(BF16), w_down=P["w_down"].astype(BF16),
        g_ple=row(P["g_ple"]), w_ple_gate=P["w_ple_gate"].astype(BF16),
        w_ple_proj=P["w_ple_proj"].astype(BF16),
    )


def _tile(n, want):
    t = min(n, want)
    assert n % t == 0
    return t


def _trunk(x, p, layers, bias, e_f, e_b, g_final):
    b, l, _ = x.shape
    t = b * l
    tm = _tile(t, 512)
    seq_tile = _tile(l, 512)
    cps = seq_tile // CHUNK

    def layer(xf, lw):
        W, pl_i = lw
        qkv, z, xbc, gag, dt = _inproj(xf, W["g_mix"], W["w_in"], tm)
        att = _attention(qkv.reshape(b, l, QKV_COLS), W["sink"], bias, seq_tile)
        xs, bc = _conv(xbc.reshape(b, l, CONV_DIM), W["conv_w"], W["conv_b"], seq_tile)
        yf, yb = _ssd(xs, bc, dt.reshape(b, l, DT_COLS), W["par"], e_f, e_b, cps)
        x1, h2g, grp, rank, counts = _merge(
            yf.reshape(t, D_INNER), yb.reshape(t, D_INNER), xs.reshape(t, D_INNER), z, att.reshape(t, Q_DIM), gag,
            xf, W["d_skip"], W["ssd_norm_g"], W["w_ssd_out"], W["w_attn_out"], W["w_out"], W["g_ffn"],
            W["wr_hi"], W["wr_lo"], W["rb"], tm)
        pos, tile_group, tile_valid, t_pad = _route(grp, rank, counts, t, tm)
        hs = _permute(h2g, pos, t_pad, tm)
        ys = _experts(hs, tile_group, tile_valid, W["w_gate_up"], W["w_down"], tm)
        x3 = _ple(x1, ys, pos, pl_i, W["g_ple"], W["w_ple_gate"], W["w_ple_proj"], tm)
        return x3, None

    depth = p.shape[0]
    xf, _ = lax.scan(layer, x.reshape(t, D_MODEL), (layers, p.reshape(depth, t, PLE_DIM)))
    return _final_norm(xf, g_final.astype(F32)[None, :], tm).reshape(b, l, D_MODEL)


def kernel(x_prompt, x_sample, p_prompt, p_sample, g_mix, w_in, rel_bias, attn_sink, w_attn_out, conv_w, conv_b, a_log_f, a_log_b, dt_bias_f, dt_bias_b, d_skip, ssd_norm_g, w_ssd_out, w_out, g_ffn, router_g, router_g_b, router_e, router_e_b, w_gate_up, w_down, g_ple, w_ple_gate, w_ple_proj, g_final):
    P = dict(g_mix=g_mix, w_in=w_in, attn_sink=attn_sink, w_attn_out=w_attn_out, conv_w=conv_w, conv_b=conv_b,
             a_log_f=a_log_f, a_log_b=a_log_b, dt_bias_f=dt_bias_f, dt_bias_b=dt_bias_b, d_skip=d_skip,
             ssd_norm_g=ssd_norm_g, w_ssd_out=w_ssd_out, w_out=w_out, g_ffn=g_ffn, router_g=router_g,
             router_g_b=router_g_b, router_e=router_e, router_e_b=router_e_b, w_gate_up=w_gate_up,
             w_down=w_down, g_ple=g_ple, w_ple_gate=w_ple_gate, w_ple_proj=w_ple_proj)
    layers = _prep_layers(P)
    bias = _band_bias(rel_bias)
    e_f = _head_expand_matrix(0)
    e_b = _head_expand_matrix(N_SSD_HEADS)
    y_prompt = _trunk(x_prompt, p_prompt, layers, bias, e_f, e_b, g_final)
    y_sample = _trunk(x_sample, p_sample, layers, bias, e_f, e_b, g_final)
    return (y_prompt, y_sample)
```

```python
import functools

import numpy as np
import jax
import jax.numpy as jnp
from jax import lax
from jax.experimental import pallas as pl
from jax.experimental.pallas import tpu as pltpu

F32 = jnp.float32
BF16 = jnp.bfloat16

LANE = 128
SUBLANES = 8
HALF_LANE = LANE // 2
VMEM_LIMIT_BYTES = 56 * 1024 * 1024

D_MODEL = 1024
HEAD_DIM = 64
N_HEADS = 8
N_KV_HEADS = 2
GQ = N_HEADS // N_KV_HEADS
Q_DIM = N_HEADS * HEAD_DIM
KV_DIM = N_KV_HEADS * HEAD_DIM
WINDOW = 128
BLK = 128
N_BUCKETS = 32
MAX_DISTANCE = 128
D_INNER = 1024
SSD_HEAD_DIM = 64
N_SSD_HEADS = D_INNER // SSD_HEAD_DIM
N_SSD_GROUPS = 2
D_STATE = 64
CHUNK = 128
CONV_WIDTH = 5
CONV_PAD = CONV_WIDTH // 2
BC_DIM = N_SSD_GROUPS * D_STATE
CONV_DIM = D_INNER + 2 * BC_DIM
N_EGROUPS = 4
EXPERTS_PER_GROUP = 4
N_EXPERTS = N_EGROUPS * EXPERTS_PER_GROUP
PAIRS_PER_GROUP = EXPERTS_PER_GROUP * (EXPERTS_PER_GROUP - 1) // 2
N_BUCKETS_MOE = N_EGROUPS * PAIRS_PER_GROUP
BUCKET_ROWS = 32
D_EXPERT = 256
PLE_DIM = 256
EPS = 1e-6
MASKED = -1e30

QKV_COLS = Q_DIM + 2 * KV_DIM
GATE_COLS = 2 * D_MODEL
DT_COLS = LANE
IN_COLS_PADDED = QKV_COLS + D_INNER + CONV_DIM + GATE_COLS + DT_COLS
HALO = 16
ROUTED_WORDS = D_MODEL // 2


def _cparams(*sem):
    return pltpu.CompilerParams(dimension_semantics=sem, vmem_limit_bytes=VMEM_LIMIT_BYTES)


def _dot(a, b):
    return jnp.dot(a, b, preferred_element_type=F32)


def _dot_nt(a, b):
    return lax.dot_general(a, b, (((1,), (1,)), ((), ())), preferred_element_type=F32)


def _split_bf16(v, terms):
    parts = []
    r = v
    for _ in range(terms):
        p = r.astype(BF16)
        parts.append(p)
        r = r - p.astype(F32)
    return parts


def _rmsnorm(x, g):
    return x * lax.rsqrt(jnp.mean(x * x, axis=-1, keepdims=True) + EPS) * g


def _sigmoid(x):
    return 1.0 / (1.0 + jnp.exp(-x))


def _inproj_kernel(x_ref, g_ref, w_ref, qkv_ref, z_ref, xbc_ref, gag_ref, dt_ref):
    hb = _rmsnorm(x_ref[...], g_ref[...]).astype(BF16)
    off = 0
    for ref in (qkv_ref, z_ref, xbc_ref, gag_ref, dt_ref):
        n = ref.shape[-1]
        ref[...] = _dot(hb, w_ref[:, off:off + n]).astype(ref.dtype)
        off += n


def _inproj(x, g, w, tm):
    t = x.shape[0]
    widths = (QKV_COLS, D_INNER, CONV_DIM, GATE_COLS, DT_COLS)
    dtypes = (BF16, BF16, BF16, BF16, F32)
    return pl.pallas_call(
        _inproj_kernel,
        grid=(t // tm,),
        in_specs=[
            pl.BlockSpec((tm, D_MODEL), lambda i: (i, 0)),
            pl.BlockSpec((1, D_MODEL), lambda i: (0, 0)),
            pl.BlockSpec((D_MODEL, IN_COLS_PADDED), lambda i: (0, 0)),
        ],
        out_specs=[pl.BlockSpec((tm, n), lambda i: (i, 0)) for n in widths],
        out_shape=[jax.ShapeDtypeStruct((t, n), d) for n, d in zip(widths, dtypes)],
        compiler_params=_cparams("parallel"),
        name="inproj",
    )(x, g, w)


STACK_HEAD_ORDER = tuple(half * GQ + p for p in range(GQ) for half in range(N_KV_HEADS))


SUBS_PER_TRIP = 2


def _attn_kernel(q_ref, kp_ref, km_ref, kn_ref, vp_ref, vm_ref, vn_ref, bias_ref, sink_ref, o_ref,
                 k_ext, v_ext, *, qb, nq):
    n = pl.program_id(1)
    k_ext[0:BLK] = kp_ref[0]
    k_ext[BLK:BLK + qb] = km_ref[0]
    k_ext[BLK + qb:] = kn_ref[0]
    v_ext[0:BLK, 0:KV_DIM] = vp_ref[0]
    v_ext[BLK:BLK + qb, 0:KV_DIM] = vm_ref[0]
    v_ext[BLK + qb:, 0:KV_DIM] = vn_ref[0]
    v_ext[:, KV_DIM:] = jnp.ones((qb + 2 * BLK, LANE), v_ext.dtype)
    low = lax.broadcasted_iota(jnp.int32, (BLK, LANE), 1) < HALF_LANE
    last_sub = qb // BLK - 1

    def sub_block(j):
        r0 = pl.multiple_of(j * BLK, BLK)
        q = q_ref[0, pl.ds(r0, BLK), :]
        zero = jnp.zeros((BLK, LANE), q.dtype)
        parts = []
        for p in range(GQ):
            qp = q[:, p * LANE:(p + 1) * LANE]
            parts += [jnp.where(low, qp, zero), jnp.where(low, zero, qp)]
        qs = jnp.concatenate(parts, axis=0)
        first = jnp.logical_and(n == 0, j == 0).astype(jnp.int32)
        last = jnp.logical_and(n == nq - 1, j == last_sub).astype(jnp.int32)
        s = _dot_nt(qs, k_ext[pl.ds(r0, 3 * BLK), :]) + bias_ref[first + 2 * last]
        sc = [s[:, c * BLK:(c + 1) * BLK] for c in range(3)]
        sk = sink_ref[...]
        row_max = jnp.max(jnp.maximum(jnp.maximum(sc[0], sc[1]), sc[2]), axis=-1, keepdims=True)
        m = jnp.maximum(row_max, sk)
        p_un = jnp.concatenate([jnp.exp(c - m) for c in sc], axis=1).astype(BF16)
        ov = _dot(p_un, v_ext[pl.ds(r0, 3 * BLK), :])
        o = ov[:, 0:KV_DIM] / (ov[:, KV_DIM:] + jnp.exp(sk - m))
        for p in range(GQ):
            o_lo = o[(2 * p) * BLK:(2 * p + 1) * BLK]
            o_hi = o[(2 * p + 1) * BLK:(2 * p + 2) * BLK]
            o_ref[0, pl.ds(r0, BLK), p * LANE:(p + 1) * LANE] = jnp.where(low, o_lo, o_hi).astype(o_ref.dtype)

    def trip(jj, carry):
        for u in range(SUBS_PER_TRIP):
            sub_block(jj * SUBS_PER_TRIP + u)
        return carry

    lax.fori_loop(0, qb // (BLK * SUBS_PER_TRIP), trip, 0)


def _attention(qkv, sink_rows, bias, qb):
    b, l, _ = qkv.shape
    nq = l // qb
    r = qb // BLK
    nblk = l // BLK
    kcol, vcol = Q_DIM // LANE, Q_DIM // LANE + 1

    def prev_map(c):
        return lambda bi, n: (bi, jnp.maximum(n * r - 1, 0), c)

    def next_map(c):
        return lambda bi, n: (bi, jnp.minimum((n + 1) * r, nblk - 1), c)

    return pl.pallas_call(
        functools.partial(_attn_kernel, qb=qb, nq=nq),
        grid=(b, nq),
        in_specs=[
            pl.BlockSpec((1, qb, Q_DIM), lambda bi, n: (bi, n, 0)),
            pl.BlockSpec((1, BLK, KV_DIM), prev_map(kcol)),
            pl.BlockSpec((1, qb, KV_DIM), lambda bi, n: (bi, n, kcol)),
            pl.BlockSpec((1, BLK, KV_DIM), next_map(kcol)),
            pl.BlockSpec((1, BLK, KV_DIM), prev_map(vcol)),
            pl.BlockSpec((1, qb, KV_DIM), lambda bi, n: (bi, n, vcol)),
            pl.BlockSpec((1, BLK, KV_DIM), next_map(vcol)),
            pl.BlockSpec((4, N_HEADS * BLK, 3 * BLK), lambda bi, n: (0, 0, 0)),
            pl.BlockSpec((N_HEADS * BLK, LANE), lambda bi, n: (0, 0)),
        ],
        out_specs=pl.BlockSpec((1, qb, Q_DIM), lambda bi, n: (bi, n, 0)),
        out_shape=jax.ShapeDtypeStruct((b, l, Q_DIM), BF16),
        scratch_shapes=[pltpu.VMEM((qb + 2 * BLK, KV_DIM), BF16),
                        pltpu.VMEM((qb + 2 * BLK, KV_DIM + LANE), BF16)],
        compiler_params=_cparams("parallel", "parallel"),
        name="attention",
    )(qkv, qkv, qkv, qkv, qkv, qkv, qkv, bias, sink_rows)


def _conv_kernel(p_ref, m_ref, n_ref, w_ref, b_ref, xs_ref, bc_ref, ext, *, tm, nblk):
    i = pl.program_id(1)
    for cb in range(CONV_DIM // LANE):
        cs = slice(cb * LANE, (cb + 1) * LANE)
        ext[cb, 0:HALO] = jnp.where(i > 0, p_ref[0, :, cs].astype(F32), 0.0)
        ext[cb, HALO:HALO + tm] = m_ref[0, :, cs].astype(F32)
        ext[cb, HALO + tm:] = jnp.where(i < nblk - 1, n_ref[0, :, cs].astype(F32), 0.0)
    for cb in range(CONV_DIM // LANE):
        cs = slice(cb * LANE, (cb + 1) * LANE)
        acc = b_ref[:, cs] + w_ref[0:1, cs] * ext[cb, HALO - CONV_PAD:HALO - CONV_PAD + tm]
        for k in range(1, CONV_WIDTH):
            acc = acc + w_ref[k:k + 1, cs] * ext[cb, HALO - CONV_PAD + k:HALO - CONV_PAD + k + tm]
        y = (acc * _sigmoid(acc))
        if cb < D_INNER // LANE:
            xs_ref[0, :, cs] = y.astype(xs_ref.dtype)
        else:
            c0 = cb * LANE - D_INNER
            bc_ref[0, :, c0:c0 + LANE] = y.astype(bc_ref.dtype)


def _conv(xbc, w, bvec, tm):
    b, l, _ = xbc.shape
    nblk = l // tm
    r = tm // HALO
    nh = l // HALO
    return pl.pallas_call(
        functools.partial(_conv_kernel, tm=tm, nblk=nblk),
        grid=(b, nblk),
        in_specs=[
            pl.BlockSpec((1, HALO, CONV_DIM), lambda bi, i: (bi, jnp.maximum(i * r - 1, 0), 0)),
            pl.BlockSpec((1, tm, CONV_DIM), lambda bi, i: (bi, i, 0)),
            pl.BlockSpec((1, HALO, CONV_DIM), lambda bi, i: (bi, jnp.minimum((i + 1) * r, nh - 1), 0)),
            pl.BlockSpec((CONV_WIDTH, CONV_DIM), lambda bi, i: (0, 0)),
            pl.BlockSpec((1, CONV_DIM), lambda bi, i: (0, 0)),
        ],
        out_specs=[
            pl.BlockSpec((1, tm, D_INNER), lambda bi, i: (bi, i, 0)),
            pl.BlockSpec((1, tm, 2 * BC_DIM), lambda bi, i: (bi, i, 0)),
        ],
        out_shape=[jax.ShapeDtypeStruct((b, l, D_INNER), BF16),
                   jax.ShapeDtypeStruct((b, l, 2 * BC_DIM), BF16)],
        scratch_shapes=[pltpu.VMEM((CONV_DIM // LANE, tm + 2 * HALO, LANE), F32)],
        compiler_params=_cparams("parallel", "parallel"),
        name="conv_silu",
    )(xbc, xbc, xbc, w, bvec)


def _ssd_chunk(x_ref, bc_ref, dt_ref, par_ref, e_ref, y_ref, s_ref, r0, *, fwd):
    rows = lax.broadcasted_iota(jnp.int32, (CHUNK, CHUNK), 0)
    cols = lax.broadcasted_iota(jnp.int32, (CHUNK, CHUNK), 1)
    mask = rows >= cols if fwd else rows <= cols
    low = cols < HALF_LANE
    lane0 = 0 if fwd else N_SSD_HEADS

    x = dt_ref[0, pl.ds(r0, CHUNK), :] + par_ref[0:1, :]
    dt = jnp.maximum(x, 0.0) + jnp.log(1.0 + jnp.exp(-jnp.abs(x)))
    a = dt * (-jnp.exp(par_ref[1:2, :]))
    tri = mask.astype(BF16)
    acs = sum(_dot(tri, part) for part in _split_bf16(a, 3))
    last = acs[CHUNK - 1:CHUNK, :] if fwd else acs[0:1, :]
    acs_t = acs.T
    dt_t = dt.T

    bc = bc_ref[0, pl.ds(r0, CHUNK), :]
    b_all = bc[:, 0:BC_DIM]
    c_all = bc[:, BC_DIM:2 * BC_DIM]
    b_t = b_all.astype(F32).T.astype(BF16)
    zero = jnp.zeros_like(c_all)
    c_grp = [jnp.where(low, c_all, zero), jnp.where(low, zero, c_all)]
    cb = [_dot_nt(c, b_all) for c in c_grp]

    def expand(v, terms):
        return sum(_dot(part, e_ref[...]) for part in _split_bf16(v, terms))

    w_state = expand(dt * jnp.exp(last - acs), 1)
    w_off = expand(jnp.exp(acs), 1)
    decay = expand(jnp.broadcast_to(jnp.exp(last), (8, LANE)), 2)[0:1, :]

    xc = x_ref[0, pl.ds(r0, CHUNK), :]
    xw = (xc.astype(F32) * w_state).astype(BF16)
    for p in range(N_SSD_HEADS // 2):
        g = p // (N_SSD_HEADS // (2 * N_SSD_GROUPS))
        ps = slice(p * LANE, (p + 1) * LANE)
        xp = xc[:, ps]
        ys = []
        for half in range(2):
            lh = lane0 + 2 * p + half
            seg = acs[:, lh:lh + 1] - acs_t[lh:lh + 1, :]
            decay_mat = jnp.exp(jnp.where(mask, seg, MASKED))
            m = (cb[g] * decay_mat * dt_t[lh:lh + 1, :]).astype(BF16)
            ys.append(_dot(m, xp))
        state = s_ref[p]
        y_off = _dot(c_grp[g], state.astype(BF16)) * w_off[:, ps]
        y_ref[0, pl.ds(r0, CHUNK), ps] = (jnp.where(low, ys[0], ys[1]) + y_off).astype(y_ref.dtype)
        s_ref[p] = state * decay[:, ps] + _dot(b_t, xw[:, ps])


def _ssd_kernel(xf_ref, bcf_ref, dtf_ref, xb_ref, bcb_ref, dtb_ref, par_ref, ef_ref, eb_ref,
                yf_ref, yb_ref, sf_ref, sb_ref, *, cps):
    @pl.when(pl.program_id(1) == 0)
    def _():
        sf_ref[...] = jnp.zeros_like(sf_ref)
        sb_ref[...] = jnp.zeros_like(sb_ref)

    def body(j, carry):
        rf = pl.multiple_of(j * CHUNK, CHUNK)
        rb = pl.multiple_of((cps - 1 - j) * CHUNK, CHUNK)
        _ssd_chunk(xf_ref, bcf_ref, dtf_ref, par_ref, ef_ref, yf_ref, sf_ref, rf, fwd=True)
        _ssd_chunk(xb_ref, bcb_ref, dtb_ref, par_ref, eb_ref, yb_ref, sb_ref, rb, fwd=False)
        return carry

    lax.fori_loop(0, cps, body, 0)


def _ssd(xs, bc, dt, par, e_f, e_b, cps):
    b, l, _ = xs.shape
    rows = cps * CHUNK
    ns = l // rows
    fmap = lambda bi, c: (bi, c, 0)
    bmap = lambda bi, c: (bi, ns - 1 - c, 0)
    const = lambda bi, c: (0, 0)
    state = pltpu.VMEM((N_SSD_HEADS // 2, LANE, LANE), F32)
    return pl.pallas_call(
        functools.partial(_ssd_kernel, cps=cps),
        grid=(b, ns),
        in_specs=[
            pl.BlockSpec((1, rows, D_INNER), fmap),
            pl.BlockSpec((1, rows, 2 * BC_DIM), fmap),
            pl.BlockSpec((1, rows, DT_COLS), fmap),
            pl.BlockSpec((1, rows, D_INNER), bmap),
            pl.BlockSpec((1, rows, 2 * BC_DIM), bmap),
            pl.BlockSpec((1, rows, DT_COLS), bmap),
            pl.BlockSpec((8, LANE), const),
            pl.BlockSpec((LANE, D_INNER), const),
            pl.BlockSpec((LANE, D_INNER), const),
        ],
        out_specs=[pl.BlockSpec((1, rows, D_INNER), fmap), pl.BlockSpec((1, rows, D_INNER), bmap)],
        out_shape=[jax.ShapeDtypeStruct((b, l, D_INNER), BF16)] * 2,
        scratch_shapes=[state, state],
        compiler_params=_cparams("parallel", "arbitrary"),
        name="ssd_scan",
    )(xs, bc, dt, xs, bc, dt, par, e_f, e_b)


def _router_gates(lg_t):
    row = lambda j: lg_t[j:j + 1, :]
    lgrp = [row(j) for j in range(N_EGROUPS)]
    m = functools.reduce(jnp.maximum, lgrp)
    g_sel = jnp.where(lgrp[0] == m, 0, jnp.where(lgrp[1] == m, 1, jnp.where(lgrp[2] == m, 2, 3)))
    p_sel = 1.0 / sum(jnp.exp(v - m) for v in lgrp)
    s = []
    for k in range(EXPERTS_PER_GROUP):
        e = [row(N_EGROUPS + EXPERTS_PER_GROUP * g + k) for g in range(N_EGROUPS)]
        s.append(jnp.where(g_sel == 0, e[0], jnp.where(g_sel == 1, e[1], jnp.where(g_sel == 2, e[2], e[3]))))
    m1 = functools.reduce(jnp.maximum, s)
    i1 = jnp.where(s[0] == m1, 0, jnp.where(s[1] == m1, 1, jnp.where(s[2] == m1, 2, 3)))
    s2 = [jnp.where(i1 == k, -jnp.inf, s[k]) for k in range(EXPERTS_PER_GROUP)]
    m2 = functools.reduce(jnp.maximum, s2)
    i2 = jnp.where(s2[0] == m2, 0, jnp.where(s2[1] == m2, 1, jnp.where(s2[2] == m2, 2, 3)))
    t = jnp.exp(m2 - m1)
    w1 = 1.0 / (1.0 + t)
    w2 = t / (1.0 + t)
    first_is_a = i1 < i2
    a = jnp.minimum(i1, i2)
    b = jnp.maximum(i1, i2)
    gate_a = jnp.where(first_is_a, w1, w2) * p_sel
    gate_b = jnp.where(first_is_a, w2, w1) * p_sel
    pair = jnp.where(a == 0, 0, jnp.where(a == 1, 3, 5)) + b - a - 1
    bucket = g_sel * PAIRS_PER_GROUP + pair
    tm = lg_t.shape[1]
    rows = lax.broadcasted_iota(jnp.int32, (SUBLANES, tm), 0)
    gates = jnp.where(rows == 0, gate_a, jnp.where(rows == 1, gate_b, 0.0))
    return jnp.concatenate([gates, jnp.zeros((LANE - SUBLANES, tm), F32)], axis=0), bucket


def _pack_bf16_pairs(h):
    bits = lax.bitcast_convert_type(h.astype(BF16).astype(F32), jnp.uint32)
    half = h.shape[1] // 2
    return (bits[:, :half] >> 16) | (bits[:, half:] & jnp.uint32(0xFFFF0000))


def _unpack_bf16_pairs(w):
    lo = lax.bitcast_convert_type(w << 16, F32).astype(BF16)
    hi = lax.bitcast_convert_type(w & jnp.uint32(0xFFFF0000), F32).astype(BF16)
    return lo, hi


def _merge_kernel(yf_ref, yb_ref, xs_ref, z_ref, att_ref, gag_ref, x_ref, dsk_ref, ng_ref,
                  wso_ref, wao_ref, wo_ref, gf_ref, wrh_ref, wrhl_ref, rb_ref, upper_ref,
                  x1_ref, h2g_ref, grp_ref, rank_ref, counts_ref, cnt):
    @pl.when(pl.program_id(0) == 0)
    def _():
        cnt[...] = jnp.zeros_like(cnt)

    y = yf_ref[...].astype(F32) + yb_ref[...].astype(F32) + dsk_ref[...] * xs_ref[...].astype(F32)
    z = z_ref[...].astype(F32)
    y = y * (z * _sigmoid(z))
    gsz = D_INNER // N_SSD_GROUPS
    parts = []
    for g in range(N_SSD_GROUPS):
        yg = y[:, g * gsz:(g + 1) * gsz]
        parts.append(yg * lax.rsqrt(jnp.mean(yg * yg, axis=-1, keepdims=True) + EPS))
    ysn = (jnp.concatenate(parts, axis=1) * ng_ref[...]).astype(BF16)
    ys = _dot(ysn, wso_ref[...])
    ya = _dot(att_ref[...], wao_ref[...])
    ga = gag_ref[:, 0:D_MODEL].astype(F32)
    gs = gag_ref[:, D_MODEL:2 * D_MODEL].astype(F32)
    mixed = _sigmoid(ga) * ya + _sigmoid(gs) * ys
    x1 = x_ref[...] + _dot(mixed.astype(BF16), wo_ref[...])
    x1_ref[...] = x1
    h2 = _rmsnorm(x1, gf_ref[...])
    hh, hl = _split_bf16(h2, 2)
    hi_terms = _dot(hh, wrhl_ref[...])
    lg = hi_terms[:, 0:LANE] + _dot(hl, wrh_ref[...]) + hi_terms[:, LANE:] + rb_ref[...]
    gates_t, bucket = _router_gates(lg.T)
    h2g_ref[:, 0:ROUTED_WORDS] = _pack_bf16_pairs(h2)
    h2g_ref[:, ROUTED_WORDS:] = lax.bitcast_convert_type(gates_t.T, jnp.uint32)
    tm = x1.shape[0]
    onehot = (lax.broadcasted_iota(jnp.int32, (BUCKET_ROWS, tm), 0) == bucket).astype(F32)
    rank_all = _dot(onehot.astype(BF16), upper_ref[...]) + cnt[:, 0:1]
    rank_ref[0] = jnp.sum(onehot * rank_all, axis=0, keepdims=True).astype(jnp.int32)
    grp_ref[0] = bucket
    cnt[...] += jnp.sum(onehot, axis=1, keepdims=True)
    counts_ref[...] = cnt[...]


def _merge(yf, yb, xs, z, att, gag, x, dsk, ng, wso, wao, wo, gf, wrh, wrhl, rb, tm):
    t = x.shape[0]
    nt = t // tm
    upper = jnp.asarray(np.triu(np.ones((tm, tm), np.float32), 1), BF16)
    tile = lambda n: pl.BlockSpec((tm, n), lambda i: (i, 0))
    full = lambda a: pl.BlockSpec(a.shape, lambda i: (0,) * a.ndim)
    row = pl.BlockSpec((1, 1, tm), lambda i: (i, 0, 0))
    consts = (dsk, ng, wso, wao, wo, gf, wrh, wrhl, rb, upper)
    return pl.pallas_call(
        _merge_kernel,
        grid=(nt,),
        in_specs=[tile(D_INNER), tile(D_INNER), tile(D_INNER), tile(D_INNER), tile(Q_DIM),
                  tile(GATE_COLS), tile(D_MODEL)] + [full(a) for a in consts],
        out_specs=[tile(D_MODEL), tile(ROUTED_WORDS + LANE), row, row, pl.BlockSpec((BUCKET_ROWS, LANE), lambda i: (0, 0))],
        out_shape=[jax.ShapeDtypeStruct((t, D_MODEL), F32),
                   jax.ShapeDtypeStruct((t, ROUTED_WORDS + LANE), jnp.uint32),
                   jax.ShapeDtypeStruct((nt, 1, tm), jnp.int32),
                   jax.ShapeDtypeStruct((nt, 1, tm), jnp.int32),
                   jax.ShapeDtypeStruct((BUCKET_ROWS, LANE), F32)],
        scratch_shapes=[pltpu.VMEM((BUCKET_ROWS, LANE), F32)],
        compiler_params=_cparams("arbitrary"),
        name="merge_router",
    )(yf, yb, xs, z, att, gag, x, *consts)


def _dma_params(*sem):
    return pltpu.CompilerParams(dimension_semantics=sem, vmem_limit_bytes=VMEM_LIMIT_BYTES,
                                disable_bounds_checks=True)


def _permute_kernel(pos_ref, src_ref, init_ref, dst_ref, sem, *, rows):
    del init_ref

    def issue(r8, carry):
        base = pl.multiple_of(r8 * SUBLANES, SUBLANES)
        src_tile = src_ref.at[pl.ds(base, SUBLANES)]
        for k in range(SUBLANES):
            pltpu.make_async_copy(src_tile.at[pl.ds(k, 1)], dst_ref.at[pl.ds(pos_ref[0, 0, base + k], 1)],
                                  sem).start()
        return carry

    lax.fori_loop(0, rows // SUBLANES, issue, 0)
    pltpu.make_async_copy(src_ref, dst_ref.at[pl.ds(0, rows)], sem).wait()


def _permute(h2g, pos, t_pad, rows):
    t, w = h2g.shape
    nt = t // rows
    return pl.pallas_call(
        functools.partial(_permute_kernel, rows=rows),
        grid=(nt,),
        in_specs=[pl.BlockSpec((1, 1, rows), lambda i: (i, 0, 0), memory_space=pltpu.SMEM),
                  pl.BlockSpec((rows, w), lambda i: (i, 0)),
                  pl.BlockSpec(memory_space=pl.ANY)],
        out_specs=pl.BlockSpec(memory_space=pl.ANY),
        out_shape=jax.ShapeDtypeStruct((t_pad, w), h2g.dtype),
        scratch_shapes=[pltpu.SemaphoreType.DMA(())],
        input_output_aliases={2: 0},
        compiler_params=_dma_params("arbitrary"),
        name="moe_permute",
    )(pos.reshape(nt, 1, rows), h2g, jnp.zeros((t_pad, w), h2g.dtype))


def _experts_kernel(ea_ref, eb_ref, tv_ref, hs_ref, wgu_a, wd_a, wgu_b, wd_b, o_ref):
    del ea_ref, eb_ref
    i = pl.program_id(0)

    @pl.when(tv_ref[i] == 0)
    def _():
        o_ref[...] = jnp.zeros_like(o_ref)

    @pl.when(tv_ref[i] != 0)
    def _():
        lo, hi = _unpack_bf16_pairs(hs_ref[:, 0:ROUTED_WORDS])
        gates = lax.bitcast_convert_type(hs_ref[:, ROUTED_WORDS:], F32)
        acc = jnp.zeros(o_ref.shape, F32)
        for lane, (wgu, wd) in enumerate(((wgu_a, wd_a), (wgu_b, wd_b))):
            gu = _dot(lo, wgu[0, 0:ROUTED_WORDS, :]) + _dot(hi, wgu[0, ROUTED_WORDS:, :])
            g = gu[:, 0:D_EXPERT]
            u = gu[:, D_EXPERT:2 * D_EXPERT]
            a = (g * _sigmoid(g)) * u * gates[:, lane:lane + 1]
            acc = acc + _dot(a.astype(BF16), wd[0])
        o_ref[...] = acc


def _experts(hs, expert_a, expert_b, tile_valid, wgu, wd, tms):
    t_pad, w = hs.shape
    pick_a = lambda i, ea, eb, tv: (ea[i], 0, 0)
    pick_b = lambda i, ea, eb, tv: (eb[i], 0, 0)
    return pl.pallas_call(
        _experts_kernel,
        grid_spec=pltpu.PrefetchScalarGridSpec(
            num_scalar_prefetch=3,
            grid=(t_pad // tms,),
            in_specs=[
                pl.BlockSpec((tms, w), lambda i, ea, eb, tv: (i, 0)),
                pl.BlockSpec((1, D_MODEL, 2 * D_EXPERT), pick_a),
                pl.BlockSpec((1, D_EXPERT, D_MODEL), pick_a),
                pl.BlockSpec((1, D_MODEL, 2 * D_EXPERT), pick_b),
                pl.BlockSpec((1, D_EXPERT, D_MODEL), pick_b),
            ],
            out_specs=pl.BlockSpec((tms, D_MODEL), lambda i, ea, eb, tv: (i, 0)),
        ),
        out_shape=jax.ShapeDtypeStruct((t_pad, D_MODEL), F32),
        compiler_params=_cparams("arbitrary"),
        name="moe_experts",
    )(expert_a, expert_b, tile_valid, hs, wgu, wd, wgu, wd)


def _ple_kernel(pos_ref, posn_ref, x_ref, p_ref, g_ref, wg_ref, wp_ref, ys_ref, o_ref, buf, sem, *, tm, nt):
    i = pl.program_id(0)
    slot = i % 2

    def gather(pref, s):
        def issue(r8, carry):
            base = pl.multiple_of(r8 * SUBLANES, SUBLANES)
            dst_tile = buf.at[s, pl.ds(base, SUBLANES)]
            for k in range(SUBLANES):
                pltpu.make_async_copy(ys_ref.at[pl.ds(pref[0, 0, base + k], 1)], dst_tile.at[pl.ds(k, 1)],
                                      sem.at[s]).start()
            return carry
        lax.fori_loop(0, tm // SUBLANES, issue, 0)

    @pl.when(i == 0)
    def _():
        gather(pos_ref, 0)

    @pl.when(i + 1 < nt)
    def _():
        gather(posn_ref, 1 - slot)

    pltpu.make_async_copy(ys_ref.at[pl.ds(0, tm)], buf.at[slot], sem.at[slot]).wait()
    x = x_ref[...] + buf[slot]
    gate = _sigmoid(_dot(_rmsnorm(x, g_ref[...]).astype(BF16), wg_ref[...]))
    o_ref[...] = x + gate * _dot(p_ref[...].astype(BF16), wp_ref[...])


def _ple(x, ys, pos, p, g, wg, wp, tm):
    t = x.shape[0]
    nt = t // tm
    pos3 = pos.reshape(nt, 1, tm)
    return pl.pallas_call(
        functools.partial(_ple_kernel, tm=tm, nt=nt),
        grid=(nt,),
        in_specs=[
            pl.BlockSpec((1, 1, tm), lambda i: (i, 0, 0), memory_space=pltpu.SMEM),
            pl.BlockSpec((1, 1, tm), lambda i: (jnp.minimum(i + 1, nt - 1), 0, 0), memory_space=pltpu.SMEM),
            pl.BlockSpec((tm, D_MODEL), lambda i: (i, 0)),
            pl.BlockSpec((tm, PLE_DIM), lambda i: (i, 0)),
            pl.BlockSpec((1, D_MODEL), lambda i: (0, 0)),
            pl.BlockSpec((D_MODEL, D_MODEL), lambda i: (0, 0)),
            pl.BlockSpec((PLE_DIM, D_MODEL), lambda i: (0, 0)),
            pl.BlockSpec(memory_space=pl.ANY),
        ],
        out_specs=pl.BlockSpec((tm, D_MODEL), lambda i: (i, 0)),
        out_shape=jax.ShapeDtypeStruct((t, D_MODEL), F32),
        scratch_shapes=[pltpu.VMEM((2, tm, D_MODEL), F32), pltpu.SemaphoreType.DMA((2,))],
        compiler_params=_dma_params("arbitrary"),
        name="ple_gate",
    )(pos3, pos3, x, p, g, wg, wp, ys)


def _route(bucket, rank, counts, t, tms):
    cnt = counts[:N_BUCKETS_MOE, 0].astype(jnp.int32)
    padded = (cnt + tms - 1) // tms * tms
    ends = jnp.cumsum(padded)
    starts = ends - padded
    bucket = bucket.reshape(t)
    pos = rank.reshape(t) + sum(jnp.where(bucket == k, starts[k], 0) for k in range(N_BUCKETS_MOE))
    t_pad = t + N_BUCKETS_MOE * tms
    tile_start = jnp.arange(t_pad // tms, dtype=jnp.int32) * tms
    tile_bucket = jnp.minimum(sum((tile_start >= ends[k]).astype(jnp.int32) for k in range(N_BUCKETS_MOE)),
                              N_BUCKETS_MOE - 1)
    tile_valid = (tile_start < ends[N_BUCKETS_MOE - 1]).astype(jnp.int32)
    pairs = [(a, b) for a in range(EXPERTS_PER_GROUP) for b in range(a + 1, EXPERTS_PER_GROUP)]
    group, pair = tile_bucket // PAIRS_PER_GROUP, tile_bucket % PAIRS_PER_GROUP
    first = sum(jnp.where(pair == k, a, 0) for k, (a, _) in enumerate(pairs))
    second = sum(jnp.where(pair == k, b, 0) for k, (_, b) in enumerate(pairs))
    expert_a = (group * EXPERTS_PER_GROUP + first).astype(jnp.int32)
    expert_b = (group * EXPERTS_PER_GROUP + second).astype(jnp.int32)
    return pos, expert_a, expert_b, tile_valid, t_pad


def _final_norm_kernel(x_ref, g_ref, o_ref):
    o_ref[...] = _rmsnorm(x_ref[...], g_ref[...])


def _final_norm(x, g, tm):
    t = x.shape[0]
    return pl.pallas_call(
        _final_norm_kernel,
        grid=(t // tm,),
        in_specs=[pl.BlockSpec((tm, D_MODEL), lambda i: (i, 0)), pl.BlockSpec((1, D_MODEL), lambda i: (0, 0))],
        out_specs=pl.BlockSpec((tm, D_MODEL), lambda i: (i, 0)),
        out_shape=jax.ShapeDtypeStruct((t, D_MODEL), F32),
        compiler_params=_cparams("parallel"),
        name="final_norm",
    )(x, g)


def _band_bias(rel_bias):
    rel = (np.arange(3 * BLK)[None, :] - BLK) - np.arange(BLK)[:, None]
    nb = N_BUCKETS // 2
    max_exact = nb // 2
    ret = (rel > 0).astype(np.int32) * nb
    n = np.abs(rel)
    large = max_exact + (np.log(np.maximum(n, 1) / max_exact) / np.log(MAX_DISTANCE / max_exact)
                         * (nb - max_exact)).astype(np.int32)
    large = np.minimum(large, nb - 1)
    bucket = (ret + np.where(n < max_exact, n, large)).astype(np.int32)
    bias = jnp.transpose(rel_bias.astype(F32)[jnp.asarray(bucket)], (2, 0, 1))
    bias = jnp.where(jnp.asarray(n <= WINDOW)[None], bias, MASKED)
    bias = bias[np.asarray(STACK_HEAD_ORDER)].reshape(N_HEADS * BLK, 3 * BLK)
    key_blk = jnp.arange(3 * BLK) // BLK
    no_left = jnp.where(key_blk == 0, MASKED, bias)
    no_right = jnp.where(key_blk == 2, MASKED, bias)
    return jnp.stack([bias, no_left, no_right, jnp.where(key_blk == 2, MASKED, no_left)])


def _by_head(a, axis):
    return jnp.concatenate([lax.slice_in_dim(a, h * HEAD_DIM, (h + 1) * HEAD_DIM, axis=axis)
                            for h in STACK_HEAD_ORDER], axis=axis)


def _head_expand_matrix(first_lane):
    e = np.zeros((LANE, D_INNER), np.float32)
    for h in range(N_SSD_HEADS):
        e[first_lane + h, h * SSD_HEAD_DIM:(h + 1) * SSD_HEAD_DIM] = 1.0
    return jnp.asarray(e, BF16)


def _prep_layers(P):
    depth = P["w_in"].shape[0]
    w = P["w_in"]
    o = np.cumsum([0, Q_DIM, KV_DIM, KV_DIM, D_INNER, CONV_DIM, N_SSD_HEADS, N_SSD_HEADS, D_MODEL, D_MODEL])
    q, k, v, z, xbc, dtf, dtb, ga, gs = [w[:, :, o[i]:o[i + 1]] for i in range(9)]
    dt_pad = jnp.zeros((depth, D_MODEL, DT_COLS - 2 * N_SSD_HEADS), F32)
    w_in = jnp.concatenate([_by_head(q, 2) * (HEAD_DIM ** -0.5), k, v, z, xbc, ga, gs, dtf, dtb, dt_pad],
                           axis=-1).astype(BF16)

    pad_lanes = lambda a: jnp.pad(a, ((0, 0), (0, LANE - a.shape[-1])))
    par = jnp.stack([pad_lanes(jnp.concatenate([P["dt_bias_f"], P["dt_bias_b"]], axis=-1)),
                     pad_lanes(jnp.concatenate([P["a_log_f"], P["a_log_b"]], axis=-1))], axis=1)
    par = jnp.pad(par.astype(F32), ((0, 0), (0, 6), (0, 0)))

    wr = jnp.concatenate([P["router_g"], P["router_e"]], axis=-1)
    wr = jnp.pad(wr, ((0, 0), (0, 0), (0, LANE - wr.shape[-1])))
    wr_hi = wr.astype(BF16)
    wr_hilo = jnp.concatenate([wr_hi, (wr - wr_hi.astype(F32)).astype(BF16)], axis=-1)
    rb = pad_lanes(jnp.concatenate([P["router_g_b"], P["router_e_b"]], axis=-1))[:, None, :]

    row = lambda a: a.astype(F32)[:, None, :]
    return dict(
        g_mix=row(P["g_mix"]), w_in=w_in, sink=jnp.broadcast_to(jnp.repeat(P["attn_sink"].astype(F32)[:, np.asarray(STACK_HEAD_ORDER)], BLK, axis=1)[:, :, None],
                              (depth, N_HEADS * BLK, LANE)),
        w_attn_out=_by_head(P["w_attn_out"], 1).astype(BF16),
        conv_w=P["conv_w"].astype(F32), conv_b=row(P["conv_b"]), par=par,
        d_skip=row(jnp.repeat(P["d_skip"], SSD_HEAD_DIM, axis=-1)), ssd_norm_g=row(P["ssd_norm_g"]),
        w_ssd_out=P["w_ssd_out"].astype(BF16), w_out=P["w_out"].astype(BF16),
        g_ffn=row(P["g_ffn"]), wr_hi=wr_hi, wr_hilo=wr_hilo, rb=rb.astype(F32),
        w_gate_up=P["w_gate_up"].astype(BF16), w_down=P["w_down"].astype(BF16),
        g_ple=row(P["g_ple"]), w_ple_gate=P["w_ple_gate"].astype(BF16),
        w_ple_proj=P["w_ple_proj"].astype(BF16),
    )


def _tile(n, want):
    t = min(n, want)
    assert n % t == 0
    return t


def _trunk(x, p, layers, bias, e_f, e_b, g_final):
    b, l, _ = x.shape
    t = b * l
    tm = _tile(t, 512)
    seq_tile = _tile(l, 512)
    cps = seq_tile // CHUNK

    def layer(xf, lw):
        W, pl_i = lw
        qkv, z, xbc, gag, dt = _inproj(xf, W["g_mix"], W["w_in"], tm)
        att = _attention(qkv.reshape(b, l, QKV_COLS), W["sink"], bias, seq_tile)
        xs, bc = _conv(xbc.reshape(b, l, CONV_DIM), W["conv_w"], W["conv_b"], seq_tile)
        yf, yb = _ssd(xs, bc, dt.reshape(b, l, DT_COLS), W["par"], e_f, e_b, cps)
        x1, h2g, grp, rank, counts = _merge(
            yf.reshape(t, D_INNER), yb.reshape(t, D_INNER), xs.reshape(t, D_INNER), z, att.reshape(t, Q_DIM), gag,
            xf, W["d_skip"], W["ssd_norm_g"], W["w_ssd_out"], W["w_attn_out"], W["w_out"], W["g_ffn"],
            W["wr_hi"], W["wr_hilo"], W["rb"], tm)
        pos, expert_a, expert_b, tile_valid, t_pad = _route(grp, rank, counts, t, tm)
        hs = _permute(h2g, pos, t_pad, tm)
        ys = _experts(hs, expert_a, expert_b, tile_valid, W["w_gate_up"], W["w_down"], tm)
        x3 = _ple(x1, ys, pos, pl_i, W["g_ple"], W["w_ple_gate"], W["w_ple_proj"], tm)
        return x3, None

    depth = p.shape[0]
    xf, _ = lax.scan(layer, x.reshape(t, D_MODEL), (layers, p.reshape(depth, t, PLE_DIM)))
    return _final_norm(xf, g_final.astype(F32)[None, :], tm).reshape(b, l, D_MODEL)


def kernel(x_prompt, x_sample, p_prompt, p_sample, g_mix, w_in, rel_bias, attn_sink, w_attn_out, conv_w, conv_b, a_log_f, a_log_b, dt_bias_f, dt_bias_b, d_skip, ssd_norm_g, w_ssd_out, w_out, g_ffn, router_g, router_g_b, router_e, router_e_b, w_gate_up, w_down, g_ple, w_ple_gate, w_ple_proj, g_final):
    P = dict(g_mix=g_mix, w_in=w_in, attn_sink=attn_sink, w_attn_out=w_attn_out, conv_w=conv_w, conv_b=conv_b,
             a_log_f=a_log_f, a_log_b=a_log_b, dt_bias_f=dt_bias_f, dt_bias_b=dt_bias_b, d_skip=d_skip,
             ssd_norm_g=ssd_norm_g, w_ssd_out=w_ssd_out, w_out=w_out, g_ffn=g_ffn, router_g=router_g,
             router_g_b=router_g_b, router_e=router_e, router_e_b=router_e_b, w_gate_up=w_gate_up,
             w_down=w_down, g_ple=g_ple, w_ple_gate=w_ple_gate, w_ple_proj=w_ple_proj)
    layers = _prep_layers(P)
    bias = _band_bias(rel_bias)
    e_f = _head_expand_matrix(0)
    e_b = _head_expand_matrix(N_SSD_HEADS)
    y_prompt = _trunk(x_prompt, p_prompt, layers, bias, e_f, e_b, g_final)
    y_sample = _trunk(x_sample, p_sample, layers, bias, e_f, e_b, g_final)
    return (y_prompt, y_sample)
```

```python
import functools

import numpy as np
import jax
import jax.numpy as jnp
from jax import lax
from jax.experimental import pallas as pl
from jax.experimental.pallas import tpu as pltpu

F32 = jnp.float32
BF16 = jnp.bfloat16

LANE = 128
SUBLANES = 8
HALF_LANE = LANE // 2
VMEM_LIMIT_BYTES = 56 * 1024 * 1024

D_MODEL = 1024
HEAD_DIM = 64
N_HEADS = 8
N_KV_HEADS = 2
GQ = N_HEADS // N_KV_HEADS
Q_DIM = N_HEADS * HEAD_DIM
KV_DIM = N_KV_HEADS * HEAD_DIM
WINDOW = 128
BLK = 128
N_BUCKETS = 32
MAX_DISTANCE = 128
D_INNER = 1024
SSD_HEAD_DIM = 64
N_SSD_HEADS = D_INNER // SSD_HEAD_DIM
N_SSD_GROUPS = 2
D_STATE = 64
CHUNK = 128
CONV_WIDTH = 5
CONV_PAD = CONV_WIDTH // 2
BC_DIM = N_SSD_GROUPS * D_STATE
CONV_DIM = D_INNER + 2 * BC_DIM
N_EGROUPS = 4
EXPERTS_PER_GROUP = 4
N_EXPERTS = N_EGROUPS * EXPERTS_PER_GROUP
PAIRS_PER_GROUP = EXPERTS_PER_GROUP * (EXPERTS_PER_GROUP - 1) // 2
N_BUCKETS_MOE = N_EGROUPS * PAIRS_PER_GROUP
BUCKET_ROWS = 32
D_EXPERT = 256
PLE_DIM = 256
EPS = 1e-6
MASKED = -1e30
LOG2_E = 1.4426950408889634

QKV_COLS = Q_DIM + 2 * KV_DIM
GATE_COLS = 2 * D_MODEL
DT_COLS = LANE
IN_COLS_PADDED = QKV_COLS + D_INNER + CONV_DIM + GATE_COLS + DT_COLS
HALO = 16
ROUTED_WORDS = D_MODEL // 2


def _cparams(*sem):
    return pltpu.CompilerParams(dimension_semantics=sem, vmem_limit_bytes=VMEM_LIMIT_BYTES)


def _layer_block(shape, layer):
    return pl.BlockSpec((None,) + tuple(shape), lambda *_: (layer,) + (0,) * len(shape))


def _dot(a, b):
    return jnp.dot(a, b, preferred_element_type=F32)


def _dot_nt(a, b):
    return lax.dot_general(a, b, (((1,), (1,)), ((), ())), preferred_element_type=F32)


def _split_bf16(v, terms):
    parts = []
    r = v
    for _ in range(terms):
        p = r.astype(BF16)
        parts.append(p)
        r = r - p.astype(F32)
    return parts


def _rmsnorm(x, g):
    return x * lax.rsqrt(jnp.mean(x * x, axis=-1, keepdims=True) + EPS) * g


def _sigmoid(x):
    return 1.0 / (1.0 + jnp.exp(-x))


def _inproj_kernel(x_ref, g_ref, w_ref, qkv_ref, z_ref, xbc_ref, gag_ref, dt_ref):
    hb = _rmsnorm(x_ref[...], g_ref[...]).astype(BF16)
    off = 0
    for ref in (qkv_ref, z_ref, xbc_ref, gag_ref, dt_ref):
        n = ref.shape[-1]
        ref[...] = _dot(hb, w_ref[:, off:off + n]).astype(ref.dtype)
        off += n


def _inproj(x, g, w, layer, tm):
    t = x.shape[0]
    widths = (QKV_COLS, D_INNER, CONV_DIM, GATE_COLS, DT_COLS)
    dtypes = (BF16, BF16, BF16, BF16, F32)
    return pl.pallas_call(
        _inproj_kernel,
        grid=(t // tm,),
        in_specs=[
            pl.BlockSpec((tm, D_MODEL), lambda i: (i, 0)),
            _layer_block((1, D_MODEL), layer),
            _layer_block((D_MODEL, IN_COLS_PADDED), layer),
        ],
        out_specs=[pl.BlockSpec((tm, n), lambda i: (i, 0)) for n in widths],
        out_shape=[jax.ShapeDtypeStruct((t, n), d) for n, d in zip(widths, dtypes)],
        compiler_params=_cparams("parallel"),
        name="inproj",
    )(x, g, w)


STACK_HEAD_ORDER = tuple(half * GQ + p for p in range(GQ) for half in range(N_KV_HEADS))


SUBS_PER_TRIP = 2


def _attn_kernel(q_ref, kp_ref, km_ref, kn_ref, vp_ref, vm_ref, vn_ref, bias_ref, sink_ref, o_ref,
                 k_ext, v_ext, *, qb, nq):
    n = pl.program_id(1)
    k_ext[0:BLK] = kp_ref[0]
    k_ext[BLK:BLK + qb] = km_ref[0]
    k_ext[BLK + qb:] = kn_ref[0]
    v_ext[0:BLK, 0:KV_DIM] = vp_ref[0]
    v_ext[BLK:BLK + qb, 0:KV_DIM] = vm_ref[0]
    v_ext[BLK + qb:, 0:KV_DIM] = vn_ref[0]
    v_ext[:, KV_DIM:] = jnp.ones((qb + 2 * BLK, LANE), v_ext.dtype)
    low = lax.broadcasted_iota(jnp.int32, (BLK, LANE), 1) < HALF_LANE
    last_sub = qb // BLK - 1

    def sub_block(j):
        r0 = pl.multiple_of(j * BLK, BLK)
        q = q_ref[0, pl.ds(r0, BLK), :]
        zero = jnp.zeros((BLK, LANE), q.dtype)
        parts = []
        for p in range(GQ):
            qp = q[:, p * LANE:(p + 1) * LANE]
            parts += [jnp.where(low, qp, zero), jnp.where(low, zero, qp)]
        qs = jnp.concatenate(parts, axis=0)
        first = jnp.logical_and(n == 0, j == 0).astype(jnp.int32)
        last = jnp.logical_and(n == nq - 1, j == last_sub).astype(jnp.int32)
        s = _dot_nt(qs, k_ext[pl.ds(r0, 3 * BLK), :]) + bias_ref[first + 2 * last]
        sc = [s[:, c * BLK:(c + 1) * BLK] for c in range(3)]
        sk = sink_ref[...]
        row_max = jnp.max(jnp.maximum(jnp.maximum(sc[0], sc[1]), sc[2]), axis=-1, keepdims=True)
        m = jnp.maximum(row_max, sk)
        p_un = jnp.concatenate([jnp.exp(c - m) for c in sc], axis=1).astype(BF16)
        ov = _dot(p_un, v_ext[pl.ds(r0, 3 * BLK), :])
        o = ov[:, 0:KV_DIM] / (ov[:, KV_DIM:] + jnp.exp(sk - m))
        for p in range(GQ):
            o_lo = o[(2 * p) * BLK:(2 * p + 1) * BLK]
            o_hi = o[(2 * p + 1) * BLK:(2 * p + 2) * BLK]
            o_ref[0, pl.ds(r0, BLK), p * LANE:(p + 1) * LANE] = jnp.where(low, o_lo, o_hi).astype(o_ref.dtype)

    def trip(jj, carry):
        for u in range(SUBS_PER_TRIP):
            sub_block(jj * SUBS_PER_TRIP + u)
        return carry

    lax.fori_loop(0, qb // (BLK * SUBS_PER_TRIP), trip, 0)


def _attention(qkv, sink_rows, bias, layer, qb):
    b, l, _ = qkv.shape
    nq = l // qb
    r = qb // BLK
    nblk = l // BLK
    kcol, vcol = Q_DIM // LANE, Q_DIM // LANE + 1

    def prev_map(c):
        return lambda bi, n: (bi, jnp.maximum(n * r - 1, 0), c)

    def next_map(c):
        return lambda bi, n: (bi, jnp.minimum((n + 1) * r, nblk - 1), c)

    return pl.pallas_call(
        functools.partial(_attn_kernel, qb=qb, nq=nq),
        grid=(b, nq),
        in_specs=[
            pl.BlockSpec((1, qb, Q_DIM), lambda bi, n: (bi, n, 0)),
            pl.BlockSpec((1, BLK, KV_DIM), prev_map(kcol)),
            pl.BlockSpec((1, qb, KV_DIM), lambda bi, n: (bi, n, kcol)),
            pl.BlockSpec((1, BLK, KV_DIM), next_map(kcol)),
            pl.BlockSpec((1, BLK, KV_DIM), prev_map(vcol)),
            pl.BlockSpec((1, qb, KV_DIM), lambda bi, n: (bi, n, vcol)),
            pl.BlockSpec((1, BLK, KV_DIM), next_map(vcol)),
            pl.BlockSpec((4, N_HEADS * BLK, 3 * BLK), lambda bi, n: (0, 0, 0)),
            _layer_block((N_HEADS * BLK, LANE), layer),
        ],
        out_specs=pl.BlockSpec((1, qb, Q_DIM), lambda bi, n: (bi, n, 0)),
        out_shape=jax.ShapeDtypeStruct((b, l, Q_DIM), BF16),
        scratch_shapes=[pltpu.VMEM((qb + 2 * BLK, KV_DIM), BF16),
                        pltpu.VMEM((qb + 2 * BLK, KV_DIM + LANE), BF16)],
        compiler_params=_cparams("parallel", "parallel"),
        name="attention",
    )(qkv, qkv, qkv, qkv, qkv, qkv, qkv, bias, sink_rows)


def _conv_kernel(p_ref, m_ref, n_ref, w_ref, b_ref, xs_ref, bc_ref, ext, *, tm, nblk):
    i = pl.program_id(1)
    for cb in range(CONV_DIM // LANE):
        cs = slice(cb * LANE, (cb + 1) * LANE)
        ext[cb, 0:HALO] = jnp.where(i > 0, p_ref[0, :, cs].astype(F32), 0.0)
        ext[cb, HALO:HALO + tm] = m_ref[0, :, cs].astype(F32)
        ext[cb, HALO + tm:] = jnp.where(i < nblk - 1, n_ref[0, :, cs].astype(F32), 0.0)
    for cb in range(CONV_DIM // LANE):
        cs = slice(cb * LANE, (cb + 1) * LANE)
        acc = b_ref[:, cs] + w_ref[0:1, cs] * ext[cb, HALO - CONV_PAD:HALO - CONV_PAD + tm]
        for k in range(1, CONV_WIDTH):
            acc = acc + w_ref[k:k + 1, cs] * ext[cb, HALO - CONV_PAD + k:HALO - CONV_PAD + k + tm]
        y = (acc * _sigmoid(acc))
        if cb < D_INNER // LANE:
            xs_ref[0, :, cs] = y.astype(xs_ref.dtype)
        else:
            c0 = cb * LANE - D_INNER
            bc_ref[0, :, c0:c0 + LANE] = y.astype(bc_ref.dtype)


def _conv(xbc, w, bvec, layer, tm):
    b, l, _ = xbc.shape
    nblk = l // tm
    r = tm // HALO
    nh = l // HALO
    return pl.pallas_call(
        functools.partial(_conv_kernel, tm=tm, nblk=nblk),
        grid=(b, nblk),
        in_specs=[
            pl.BlockSpec((1, HALO, CONV_DIM), lambda bi, i: (bi, jnp.maximum(i * r - 1, 0), 0)),
            pl.BlockSpec((1, tm, CONV_DIM), lambda bi, i: (bi, i, 0)),
            pl.BlockSpec((1, HALO, CONV_DIM), lambda bi, i: (bi, jnp.minimum((i + 1) * r, nh - 1), 0)),
            _layer_block((CONV_WIDTH, CONV_DIM), layer),
            _layer_block((1, CONV_DIM), layer),
        ],
        out_specs=[
            pl.BlockSpec((1, tm, D_INNER), lambda bi, i: (bi, i, 0)),
            pl.BlockSpec((1, tm, 2 * BC_DIM), lambda bi, i: (bi, i, 0)),
        ],
        out_shape=[jax.ShapeDtypeStruct((b, l, D_INNER), BF16),
                   jax.ShapeDtypeStruct((b, l, 2 * BC_DIM), BF16)],
        scratch_shapes=[pltpu.VMEM((CONV_DIM // LANE, tm + 2 * HALO, LANE), F32)],
        compiler_params=_cparams("parallel", "parallel"),
        name="conv_silu",
    )(xbc, xbc, xbc, w, bvec)


def _ssd_chunk(x_ref, bc_ref, dt_ref, par_ref, e_ref, y_ref, s_ref, r0, *, fwd):
    rows = lax.broadcasted_iota(jnp.int32, (CHUNK, CHUNK), 0)
    cols = lax.broadcasted_iota(jnp.int32, (CHUNK, CHUNK), 1)
    mask = rows >= cols if fwd else rows <= cols
    low = cols < HALF_LANE
    lane0 = 0 if fwd else N_SSD_HEADS

    x = dt_ref[0, pl.ds(r0, CHUNK), :] + par_ref[0:1, :]
    dt = jnp.maximum(x, 0.0) + jnp.log(1.0 + jnp.exp(-jnp.abs(x)))
    a = dt * (-jnp.exp(par_ref[1:2, :]))
    tri = mask.astype(BF16)
    acs = sum(_dot(tri, part) for part in _split_bf16(a, 3))
    last = acs[CHUNK - 1:CHUNK, :] if fwd else acs[0:1, :]
    acs2 = acs * LOG2_E
    acs2_t = acs2.T
    dt_t = dt.T

    bc = bc_ref[0, pl.ds(r0, CHUNK), :]
    b_all = bc[:, 0:BC_DIM]
    c_all = bc[:, BC_DIM:2 * BC_DIM]
    b_t = b_all.astype(F32).T.astype(BF16)
    zero = jnp.zeros_like(c_all)
    c_grp = [jnp.where(low, c_all, zero), jnp.where(low, zero, c_all)]
    cb = [_dot_nt(c, b_all) for c in c_grp]

    def expand(v, terms):
        return sum(_dot(part, e_ref[...]) for part in _split_bf16(v, terms))

    w_state = expand(dt * jnp.exp(last - acs), 1).astype(BF16)
    w_off = expand(jnp.exp(acs), 1)
    decay = expand(jnp.broadcast_to(jnp.exp(last), (8, LANE)), 2)[0:1, :]

    xc = x_ref[0, pl.ds(r0, CHUNK), :]
    xw = xc * w_state
    x_zero = jnp.zeros((CHUNK, LANE), xc.dtype)
    for p in range(N_SSD_HEADS // 2):
        g = p // (N_SSD_HEADS // (2 * N_SSD_GROUPS))
        ps = slice(p * LANE, (p + 1) * LANE)
        xp = xc[:, ps]
        ms = []
        for half in range(2):
            lh = lane0 + 2 * p + half
            seg2 = acs2[:, lh:lh + 1] - acs2_t[lh:lh + 1, :]
            decay_mat = jnp.exp2(jnp.where(mask, seg2, MASKED))
            ms.append((cb[g] * decay_mat * dt_t[lh:lh + 1, :]).astype(BF16))
        x_heads = jnp.concatenate([jnp.where(low, xp, x_zero), jnp.where(low, x_zero, xp)], axis=0)
        y_diag = _dot(jnp.concatenate(ms, axis=1), x_heads)
        state = s_ref[p]
        y_off = _dot(c_grp[g], state.astype(BF16)) * w_off[:, ps]
        y_ref[0, pl.ds(r0, CHUNK), ps] = (y_diag + y_off).astype(y_ref.dtype)
        s_ref[p] = state * decay[:, ps] + _dot(b_t, xw[:, ps])


def _ssd_kernel(xf_ref, bcf_ref, dtf_ref, xb_ref, bcb_ref, dtb_ref, par_ref, ef_ref, eb_ref,
                yf_ref, yb_ref, sf_ref, sb_ref, *, cps):
    @pl.when(pl.program_id(1) == 0)
    def _():
        sf_ref[...] = jnp.zeros_like(sf_ref)
        sb_ref[...] = jnp.zeros_like(sb_ref)

    def body(j, carry):
        rf = pl.multiple_of(j * CHUNK, CHUNK)
        rb = pl.multiple_of((cps - 1 - j) * CHUNK, CHUNK)
        _ssd_chunk(xf_ref, bcf_ref, dtf_ref, par_ref, ef_ref, yf_ref, sf_ref, rf, fwd=True)
        _ssd_chunk(xb_ref, bcb_ref, dtb_ref, par_ref, eb_ref, yb_ref, sb_ref, rb, fwd=False)
        return carry

    lax.fori_loop(0, cps, body, 0, unroll=2)


def _ssd(xs, bc, dt, par, e_f, e_b, layer, cps):
    b, l, _ = xs.shape
    rows = cps * CHUNK
    ns = l // rows
    fmap = lambda bi, c: (bi, c, 0)
    bmap = lambda bi, c: (bi, ns - 1 - c, 0)
    const = lambda bi, c: (0, 0)
    state = pltpu.VMEM((N_SSD_HEADS // 2, LANE, LANE), F32)
    return pl.pallas_call(
        functools.partial(_ssd_kernel, cps=cps),
        grid=(b, ns),
        in_specs=[
            pl.BlockSpec((1, rows, D_INNER), fmap),
            pl.BlockSpec((1, rows, 2 * BC_DIM), fmap),
            pl.BlockSpec((1, rows, DT_COLS), fmap),
            pl.BlockSpec((1, rows, D_INNER), bmap),
            pl.BlockSpec((1, rows, 2 * BC_DIM), bmap),
            pl.BlockSpec((1, rows, DT_COLS), bmap),
            _layer_block((8, LANE), layer),
            pl.BlockSpec((LANE, D_INNER), const),
            pl.BlockSpec((LANE, D_INNER), const),
        ],
        out_specs=[pl.BlockSpec((1, rows, D_INNER), fmap), pl.BlockSpec((1, rows, D_INNER), bmap)],
        out_shape=[jax.ShapeDtypeStruct((b, l, D_INNER), BF16)] * 2,
        scratch_shapes=[state, state],
        compiler_params=_cparams("parallel", "arbitrary"),
        name="ssd_scan",
    )(xs, bc, dt, xs, bc, dt, par, e_f, e_b)


def _router_gates(lg_t):
    row = lambda j: lg_t[j:j + 1, :]
    lgrp = [row(j) for j in range(N_EGROUPS)]
    m = functools.reduce(jnp.maximum, lgrp)
    g_sel = jnp.where(lgrp[0] == m, 0, jnp.where(lgrp[1] == m, 1, jnp.where(lgrp[2] == m, 2, 3)))
    p_sel = 1.0 / sum(jnp.exp(v - m) for v in lgrp)
    s = []
    for k in range(EXPERTS_PER_GROUP):
        e = [row(N_EGROUPS + EXPERTS_PER_GROUP * g + k) for g in range(N_EGROUPS)]
        s.append(jnp.where(g_sel == 0, e[0], jnp.where(g_sel == 1, e[1], jnp.where(g_sel == 2, e[2], e[3]))))
    m1 = functools.reduce(jnp.maximum, s)
    i1 = jnp.where(s[0] == m1, 0, jnp.where(s[1] == m1, 1, jnp.where(s[2] == m1, 2, 3)))
    s2 = [jnp.where(i1 == k, -jnp.inf, s[k]) for k in range(EXPERTS_PER_GROUP)]
    m2 = functools.reduce(jnp.maximum, s2)
    i2 = jnp.where(s2[0] == m2, 0, jnp.where(s2[1] == m2, 1, jnp.where(s2[2] == m2, 2, 3)))
    t = jnp.exp(m2 - m1)
    w1 = 1.0 / (1.0 + t)
    w2 = t / (1.0 + t)
    first_is_a = i1 < i2
    a = jnp.minimum(i1, i2)
    b = jnp.maximum(i1, i2)
    gate_a = jnp.where(first_is_a, w1, w2) * p_sel
    gate_b = jnp.where(first_is_a, w2, w1) * p_sel
    pair = jnp.where(a == 0, 0, jnp.where(a == 1, 3, 5)) + b - a - 1
    bucket = g_sel * PAIRS_PER_GROUP + pair
    tm = lg_t.shape[1]
    rows = lax.broadcasted_iota(jnp.int32, (SUBLANES, tm), 0)
    gates = jnp.where(rows == 0, gate_a, jnp.where(rows == 1, gate_b, 0.0))
    return jnp.concatenate([gates, jnp.zeros((LANE - SUBLANES, tm), F32)], axis=0), bucket


def _pack_bf16_pairs(h):
    bits = lax.bitcast_convert_type(h.astype(BF16).astype(F32), jnp.uint32)
    half = h.shape[1] // 2
    return (bits[:, :half] >> 16) | (bits[:, half:] & jnp.uint32(0xFFFF0000))


def _unpack_bf16_pairs(w):
    lo = lax.bitcast_convert_type(w << 16, F32).astype(BF16)
    hi = lax.bitcast_convert_type(w & jnp.uint32(0xFFFF0000), F32).astype(BF16)
    return lo, hi


def _merge_kernel(yf_ref, yb_ref, xs_ref, z_ref, att_ref, gag_ref, x_ref, dsk_ref, ng_ref,
                  wso_ref, wao_ref, wo_ref, gf_ref, wrh_ref, wrhl_ref, rb_ref, upper_ref,
                  x1_ref, h2g_ref, grp_ref, rank_ref, counts_ref, cnt):
    @pl.when(pl.program_id(0) == 0)
    def _():
        cnt[...] = jnp.zeros_like(cnt)

    y = yf_ref[...].astype(F32) + yb_ref[...].astype(F32) + dsk_ref[...] * xs_ref[...].astype(F32)
    z = z_ref[...].astype(F32)
    y = y * (z * _sigmoid(z))
    gsz = D_INNER // N_SSD_GROUPS
    parts = []
    for g in range(N_SSD_GROUPS):
        yg = y[:, g * gsz:(g + 1) * gsz]
        parts.append(yg * lax.rsqrt(jnp.mean(yg * yg, axis=-1, keepdims=True) + EPS))
    ysn = (jnp.concatenate(parts, axis=1) * ng_ref[...]).astype(BF16)
    ys = _dot(ysn, wso_ref[...])
    ya = _dot(att_ref[...], wao_ref[...])
    ga = gag_ref[:, 0:D_MODEL].astype(F32)
    gs = gag_ref[:, D_MODEL:2 * D_MODEL].astype(F32)
    mixed = _sigmoid(ga) * ya + _sigmoid(gs) * ys
    x1 = x_ref[...] + _dot(mixed.astype(BF16), wo_ref[...])
    x1_ref[...] = x1
    h2 = _rmsnorm(x1, gf_ref[...])
    h2g_ref[:, 0:ROUTED_WORDS] = _pack_bf16_pairs(h2)
    hh, hl = _split_bf16(h2, 2)
    hi_terms = _dot(hh, wrhl_ref[...])
    lg = hi_terms[:, 0:LANE] + _dot(hl, wrh_ref[...]) + hi_terms[:, LANE:] + rb_ref[...]
    tm = x1.shape[0]
    gates_t, bucket = _router_gates(lg.T)
    h2g_ref[:, ROUTED_WORDS:] = lax.bitcast_convert_type(gates_t.T, jnp.uint32)
    onehot = (lax.broadcasted_iota(jnp.int32, (BUCKET_ROWS, tm), 0) == bucket).astype(F32)
    rank_all = _dot(onehot.astype(BF16), upper_ref[...]) + cnt[:, 0:1]
    rank_ref[0] = jnp.sum(onehot * rank_all, axis=0, keepdims=True).astype(jnp.int32)
    grp_ref[0] = bucket
    cnt[...] += jnp.sum(onehot, axis=1, keepdims=True)
    counts_ref[...] = cnt[...]


def _merge(yf, yb, xs, z, att, gag, x, dsk, ng, wso, wao, wo, gf, wrh, wrhl, rb, layer, tm):
    t = x.shape[0]
    nt = t // tm
    upper = jnp.asarray(np.triu(np.ones((tm, tm), np.float32), 1), BF16)
    tile = lambda n: pl.BlockSpec((tm, n), lambda i: (i, 0))
    per_layer = lambda a: _layer_block(a.shape[1:], layer)
    row = pl.BlockSpec((1, 1, tm), lambda i: (i, 0, 0))
    consts = (dsk, ng, wso, wao, wo, gf, wrh, wrhl, rb)
    return pl.pallas_call(
        _merge_kernel,
        grid=(nt,),
        in_specs=[tile(D_INNER), tile(D_INNER), tile(D_INNER), tile(D_INNER), tile(Q_DIM),
                  tile(GATE_COLS), tile(D_MODEL)] + [per_layer(a) for a in consts]
                 + [pl.BlockSpec((tm, tm), lambda i: (0, 0))],
        out_specs=[tile(D_MODEL), tile(ROUTED_WORDS + LANE), row, row, pl.BlockSpec((BUCKET_ROWS, LANE), lambda i: (0, 0))],
        out_shape=[jax.ShapeDtypeStruct((t, D_MODEL), F32),
                   jax.ShapeDtypeStruct((t, ROUTED_WORDS + LANE), jnp.uint32),
                   jax.ShapeDtypeStruct((nt, 1, tm), jnp.int32),
                   jax.ShapeDtypeStruct((nt, 1, tm), jnp.int32),
                   jax.ShapeDtypeStruct((BUCKET_ROWS, LANE), F32)],
        scratch_shapes=[pltpu.VMEM((BUCKET_ROWS, LANE), F32)],
        compiler_params=_cparams("arbitrary"),
        name="merge_router",
    )(yf, yb, xs, z, att, gag, x, *consts, upper)


def _dma_params(*sem):
    return pltpu.CompilerParams(dimension_semantics=sem, vmem_limit_bytes=VMEM_LIMIT_BYTES,
                                disable_bounds_checks=True)


def _permute_kernel(pos_ref, src_ref, init_ref, dst_ref, stage, sem, *, rows, nt):
    del init_ref
    i = pl.program_id(0)
    slot = i % 2
    stage[slot] = src_ref[...]

    def issue(r8, carry):
        base = pl.multiple_of(r8 * SUBLANES, SUBLANES)
        src_tile = stage.at[slot, pl.ds(base, SUBLANES)]
        for k in range(SUBLANES):
            pltpu.make_async_copy(src_tile.at[pl.ds(k, 1)], dst_ref.at[pl.ds(pos_ref[0, 0, base + k], 1)],
                                  sem.at[slot]).start()
        return carry

    lax.fori_loop(0, rows // SUBLANES, issue, 0)

    def wait_all(s):
        pltpu.make_async_copy(stage.at[s], dst_ref.at[pl.ds(0, rows)], sem.at[s]).wait()

    @pl.when(i > 0)
    def _():
        wait_all(1 - slot)

    @pl.when(i == nt - 1)
    def _():
        wait_all(slot)


def _permute(h2g, pos, t_pad, rows):
    t, w = h2g.shape
    nt = t // rows
    return pl.pallas_call(
        functools.partial(_permute_kernel, rows=rows, nt=nt),
        grid=(nt,),
        in_specs=[pl.BlockSpec((1, 1, rows), lambda i: (i, 0, 0), memory_space=pltpu.SMEM),
                  pl.BlockSpec((rows, w), lambda i: (i, 0)),
                  pl.BlockSpec(memory_space=pl.ANY)],
        out_specs=pl.BlockSpec(memory_space=pl.ANY),
        out_shape=jax.ShapeDtypeStruct((t_pad, w), h2g.dtype),
        scratch_shapes=[pltpu.VMEM((2, rows, w), h2g.dtype), pltpu.SemaphoreType.DMA((2,))],
        input_output_aliases={2: 0},
        compiler_params=_dma_params("arbitrary"),
        name="moe_permute",
    )(pos.reshape(nt, 1, rows), h2g, jnp.zeros((t_pad, w), h2g.dtype))


def _experts_kernel(ea_ref, eb_ref, tv_ref, hs_ref, wgu_a, wd_a, wgu_b, wd_b, o_ref):
    del ea_ref, eb_ref
    i = pl.program_id(0)

    @pl.when(tv_ref[i] == 0)
    def _():
        o_ref[...] = jnp.zeros_like(o_ref)

    @pl.when(tv_ref[i] != 0)
    def _():
        lo, hi = _unpack_bf16_pairs(hs_ref[:, 0:ROUTED_WORDS])
        gates = lax.bitcast_convert_type(hs_ref[:, ROUTED_WORDS:], F32)
        acc = jnp.zeros(o_ref.shape, F32)
        for lane, (wgu, wd) in enumerate(((wgu_a, wd_a), (wgu_b, wd_b))):
            gu = _dot(lo, wgu[0, 0:ROUTED_WORDS, :]) + _dot(hi, wgu[0, ROUTED_WORDS:, :])
            g = gu[:, 0:D_EXPERT]
            u = gu[:, D_EXPERT:2 * D_EXPERT]
            a = (g * _sigmoid(g)) * u * gates[:, lane:lane + 1]
            acc = acc + _dot(a.astype(BF16), wd[0])
        o_ref[...] = acc


def _experts(hs, expert_a, expert_b, tile_valid, wgu, wd, layer, tms):
    t_pad, w = hs.shape
    pick_a = lambda i, ea, eb, tv: (layer, ea[i], 0, 0)
    pick_b = lambda i, ea, eb, tv: (layer, eb[i], 0, 0)
    return pl.pallas_call(
        _experts_kernel,
        grid_spec=pltpu.PrefetchScalarGridSpec(
            num_scalar_prefetch=3,
            grid=(t_pad // tms,),
            in_specs=[
                pl.BlockSpec((tms, w), lambda i, ea, eb, tv: (i, 0)),
                pl.BlockSpec((None, 1, D_MODEL, 2 * D_EXPERT), pick_a),
                pl.BlockSpec((None, 1, D_EXPERT, D_MODEL), pick_a),
                pl.BlockSpec((None, 1, D_MODEL, 2 * D_EXPERT), pick_b),
                pl.BlockSpec((None, 1, D_EXPERT, D_MODEL), pick_b),
            ],
            out_specs=pl.BlockSpec((tms, D_MODEL), lambda i, ea, eb, tv: (i, 0)),
        ),
        out_shape=jax.ShapeDtypeStruct((t_pad, D_MODEL), F32),
        compiler_params=_cparams("arbitrary"),
        name="moe_experts",
    )(expert_a, expert_b, tile_valid, hs, wgu, wd, wgu, wd)


def _ple_kernel(pos_ref, posn_ref, x_ref, p_ref, g_ref, wg_ref, wp_ref, gfin_ref, ys_ref, o_ref, buf, sem,
                *, tm, nt, final):
    i = pl.program_id(0)
    slot = i % 2

    def gather(pref, s):
        def issue(r8, carry):
            base = pl.multiple_of(r8 * SUBLANES, SUBLANES)
            dst_tile = buf.at[s, pl.ds(base, SUBLANES)]
            for k in range(SUBLANES):
                pltpu.make_async_copy(ys_ref.at[pl.ds(pref[0, 0, base + k], 1)], dst_tile.at[pl.ds(k, 1)],
                                      sem.at[s]).start()
            return carry
        lax.fori_loop(0, tm // SUBLANES, issue, 0)

    @pl.when(i == 0)
    def _():
        gather(pos_ref, 0)

    @pl.when(i + 1 < nt)
    def _():
        gather(posn_ref, 1 - slot)

    pltpu.make_async_copy(ys_ref.at[pl.ds(0, tm)], buf.at[slot], sem.at[slot]).wait()
    x = x_ref[...] + buf[slot]
    gate = _sigmoid(_dot(_rmsnorm(x, g_ref[...]).astype(BF16), wg_ref[...]))
    y = x + gate * _dot(p_ref[...].astype(BF16), wp_ref[...])
    o_ref[...] = _rmsnorm(y, gfin_ref[...]) if final else y


def _ple(x, ys, pos, p, g, wg, wp, g_final, layer, final, tm):
    t = x.shape[0]
    nt = t // tm
    pos3 = pos.reshape(nt, 1, tm)
    return pl.pallas_call(
        functools.partial(_ple_kernel, tm=tm, nt=nt, final=final),
        grid=(nt,),
        in_specs=[
            pl.BlockSpec((1, 1, tm), lambda i: (i, 0, 0), memory_space=pltpu.SMEM),
            pl.BlockSpec((1, 1, tm), lambda i: (jnp.minimum(i + 1, nt - 1), 0, 0), memory_space=pltpu.SMEM),
            pl.BlockSpec((tm, D_MODEL), lambda i: (i, 0)),
            pl.BlockSpec((None, tm, PLE_DIM), lambda i: (layer, i, 0)),
            _layer_block((1, D_MODEL), layer),
            _layer_block((D_MODEL, D_MODEL), layer),
            _layer_block((PLE_DIM, D_MODEL), layer),
            pl.BlockSpec((1, D_MODEL), lambda i: (0, 0)),
            pl.BlockSpec(memory_space=pl.ANY),
        ],
        out_specs=pl.BlockSpec((tm, D_MODEL), lambda i: (i, 0)),
        out_shape=jax.ShapeDtypeStruct((t, D_MODEL), F32),
        scratch_shapes=[pltpu.VMEM((2, tm, D_MODEL), F32), pltpu.SemaphoreType.DMA((2,))],
        compiler_params=_dma_params("arbitrary"),
        name="ple_gate",
    )(pos3, pos3, x, p, g, wg, wp, g_final, ys)


def _route(bucket, rank, counts, t, tms):
    cnt = counts[:N_BUCKETS_MOE, 0].astype(jnp.int32)
    padded = (cnt + tms - 1) // tms * tms
    ends = jnp.cumsum(padded)
    starts = ends - padded
    bucket = bucket.reshape(t)
    pos = rank.reshape(t) + sum(jnp.where(bucket == k, starts[k], 0) for k in range(N_BUCKETS_MOE))
    t_pad = t + N_BUCKETS_MOE * tms
    tile_start = jnp.arange(t_pad // tms, dtype=jnp.int32) * tms
    tile_bucket = jnp.minimum(sum((tile_start >= ends[k]).astype(jnp.int32) for k in range(N_BUCKETS_MOE)),
                              N_BUCKETS_MOE - 1)
    tile_valid = (tile_start < ends[N_BUCKETS_MOE - 1]).astype(jnp.int32)
    pairs = [(a, b) for a in range(EXPERTS_PER_GROUP) for b in range(a + 1, EXPERTS_PER_GROUP)]
    group, pair = tile_bucket // PAIRS_PER_GROUP, tile_bucket % PAIRS_PER_GROUP
    first = sum(jnp.where(pair == k, a, 0) for k, (a, _) in enumerate(pairs))
    second = sum(jnp.where(pair == k, b, 0) for k, (_, b) in enumerate(pairs))
    expert_a = (group * EXPERTS_PER_GROUP + first).astype(jnp.int32)
    expert_b = (group * EXPERTS_PER_GROUP + second).astype(jnp.int32)
    return pos, expert_a, expert_b, tile_valid, t_pad


def _band_bias(rel_bias):
    rel = (np.arange(3 * BLK)[None, :] - BLK) - np.arange(BLK)[:, None]
    nb = N_BUCKETS // 2
    max_exact = nb // 2
    ret = (rel > 0).astype(np.int32) * nb
    n = np.abs(rel)
    large = max_exact + (np.log(np.maximum(n, 1) / max_exact) / np.log(MAX_DISTANCE / max_exact)
                         * (nb - max_exact)).astype(np.int32)
    large = np.minimum(large, nb - 1)
    bucket = (ret + np.where(n < max_exact, n, large)).astype(np.int32)
    table = rel_bias.astype(F32)
    bucket = jnp.asarray(bucket)[None]
    bias = jnp.zeros((N_HEADS, BLK, 3 * BLK), F32)
    for k in range(N_BUCKETS):
        bias = jnp.where(bucket == k, table[k][:, None, None], bias)
    bias = jnp.where(jnp.asarray(n <= WINDOW)[None], bias, MASKED)
    bias = bias[np.asarray(STACK_HEAD_ORDER)].reshape(N_HEADS * BLK, 3 * BLK)
    key_blk = jnp.arange(3 * BLK) // BLK
    no_left = jnp.where(key_blk == 0, MASKED, bias)
    no_right = jnp.where(key_blk == 2, MASKED, bias)
    return jnp.stack([bias, no_left, no_right, jnp.where(key_blk == 2, MASKED, no_left)])


def _by_head(a, axis):
    return jnp.concatenate([lax.slice_in_dim(a, h * HEAD_DIM, (h + 1) * HEAD_DIM, axis=axis)
                            for h in STACK_HEAD_ORDER], axis=axis)


def _head_expand_matrix(first_lane):
    e = np.zeros((LANE, D_INNER), np.float32)
    for h in range(N_SSD_HEADS):
        e[first_lane + h, h * SSD_HEAD_DIM:(h + 1) * SSD_HEAD_DIM] = 1.0
    return jnp.asarray(e, BF16)


def _prep_layers(P):
    depth = P["w_in"].shape[0]
    w = P["w_in"]
    o = np.cumsum([0, Q_DIM, KV_DIM, KV_DIM, D_INNER, CONV_DIM, N_SSD_HEADS, N_SSD_HEADS, D_MODEL, D_MODEL])
    q, k, v, z, xbc, dtf, dtb, ga, gs = [w[:, :, o[i]:o[i + 1]] for i in range(9)]
    dt_pad = jnp.zeros((depth, D_MODEL, DT_COLS - 2 * N_SSD_HEADS), F32)
    w_in = jnp.concatenate([_by_head(q, 2) * (HEAD_DIM ** -0.5), k, v, z, xbc, ga, gs, dtf, dtb, dt_pad],
                           axis=-1).astype(BF16)

    pad_lanes = lambda a: jnp.pad(a, ((0, 0), (0, LANE - a.shape[-1])))
    par = jnp.stack([pad_lanes(jnp.concatenate([P["dt_bias_f"], P["dt_bias_b"]], axis=-1)),
                     pad_lanes(jnp.concatenate([P["a_log_f"], P["a_log_b"]], axis=-1))], axis=1)
    par = jnp.pad(par.astype(F32), ((0, 0), (0, 6), (0, 0)))

    wr = jnp.concatenate([P["router_g"], P["router_e"]], axis=-1)
    wr = jnp.pad(wr, ((0, 0), (0, 0), (0, LANE - wr.shape[-1])))
    wr_hi = wr.astype(BF16)
    wr_hilo = jnp.concatenate([wr_hi, (wr - wr_hi.astype(F32)).astype(BF16)], axis=-1)
    rb = pad_lanes(jnp.concatenate([P["router_g_b"], P["router_e_b"]], axis=-1))[:, None, :]

    row = lambda a: a.astype(F32)[:, None, :]
    return dict(
        g_mix=row(P["g_mix"]), w_in=w_in, sink=jnp.broadcast_to(jnp.repeat(P["attn_sink"].astype(F32)[:, np.asarray(STACK_HEAD_ORDER)], BLK, axis=1)[:, :, None],
                              (depth, N_HEADS * BLK, LANE)),
        w_attn_out=_by_head(P["w_attn_out"], 1).astype(BF16),
        conv_w=P["conv_w"].astype(F32), conv_b=row(P["conv_b"]), par=par,
        d_skip=row(jnp.repeat(P["d_skip"], SSD_HEAD_DIM, axis=-1)), ssd_norm_g=row(P["ssd_norm_g"]),
        w_ssd_out=P["w_ssd_out"].astype(BF16), w_out=P["w_out"].astype(BF16),
        g_ffn=row(P["g_ffn"]), wr_hi=wr_hi, wr_hilo=wr_hilo, rb=rb.astype(F32),
        w_gate_up=P["w_gate_up"].astype(BF16), w_down=P["w_down"].astype(BF16),
        g_ple=row(P["g_ple"]), w_ple_gate=P["w_ple_gate"].astype(BF16),
        w_ple_proj=P["w_ple_proj"].astype(BF16),
    )


def _tile(n, want):
    t = min(n, want)
    assert n % t == 0
    return t


def _trunk(x, p, layers, bias, e_f, e_b, g_final):
    b, l, _ = x.shape
    t = b * l
    tm = _tile(t, 512)
    seq_tile = _tile(l, 512)
    cps = seq_tile // CHUNK

    W = layers
    depth = p.shape[0]
    p = p.reshape(depth, t, PLE_DIM)
    g_final = g_final.astype(F32)[None, :]
    xf = x.reshape(t, D_MODEL)
    for i in range(depth):
        qkv, z, xbc, gag, dt = _inproj(xf, W["g_mix"], W["w_in"], i, tm)
        att = _attention(qkv.reshape(b, l, QKV_COLS), W["sink"], bias, i, seq_tile)
        xs, bc = _conv(xbc.reshape(b, l, CONV_DIM), W["conv_w"], W["conv_b"], i, seq_tile)
        yf, yb = _ssd(xs, bc, dt.reshape(b, l, DT_COLS), W["par"], e_f, e_b, i, cps)
        x1, h2g, grp, rank, counts = _merge(
            yf.reshape(t, D_INNER), yb.reshape(t, D_INNER), xs.reshape(t, D_INNER), z, att.reshape(t, Q_DIM), gag,
            xf, W["d_skip"], W["ssd_norm_g"], W["w_ssd_out"], W["w_attn_out"], W["w_out"], W["g_ffn"],
            W["wr_hi"], W["wr_hilo"], W["rb"], i, tm)
        pos, expert_a, expert_b, tile_valid, t_pad = _route(grp, rank, counts, t, tm)
        hs = _permute(h2g, pos, t_pad, tm)
        ys = _experts(hs, expert_a, expert_b, tile_valid, W["w_gate_up"], W["w_down"], i, tm)
        xf = _ple(x1, ys, pos, p, W["g_ple"], W["w_ple_gate"], W["w_ple_proj"], g_final, i, i == depth - 1, tm)
    return xf.reshape(b, l, D_MODEL)


def kernel(x_prompt, x_sample, p_prompt, p_sample, g_mix, w_in, rel_bias, attn_sink, w_attn_out, conv_w, conv_b, a_log_f, a_log_b, dt_bias_f, dt_bias_b, d_skip, ssd_norm_g, w_ssd_out, w_out, g_ffn, router_g, router_g_b, router_e, router_e_b, w_gate_up, w_down, g_ple, w_ple_gate, w_ple_proj, g_final):
    P = dict(g_mix=g_mix, w_in=w_in, attn_sink=attn_sink, w_attn_out=w_attn_out, conv_w=conv_w, conv_b=conv_b,
             a_log_f=a_log_f, a_log_b=a_log_b, dt_bias_f=dt_bias_f, dt_bias_b=dt_bias_b, d_skip=d_skip,
             ssd_norm_g=ssd_norm_g, w_ssd_out=w_ssd_out, w_out=w_out, g_ffn=g_ffn, router_g=router_g,
             router_g_b=router_g_b, router_e=router_e, router_e_b=router_e_b, w_gate_up=w_gate_up,
             w_down=w_down, g_ple=g_ple, w_ple_gate=w_ple_gate, w_ple_proj=w_ple_proj)
    layers = _prep_layers(P)
    bias = _band_bias(rel_bias)
    e_f = _head_expand_matrix(0)
    e_b = _head_expand_matrix(N_SSD_HEADS)
    y_prompt = _trunk(x_prompt, p_prompt, layers, bias, e_f, e_b, g_final)
    y_sample = _trunk(x_sample, p_sample, layers, bias, e_f, e_b, g_final)
    return (y_prompt, y_sample)
```

```python
import functools

import numpy as np
import jax
import jax.numpy as jnp
from jax import lax
from jax.experimental import pallas as pl
from jax.experimental.pallas import tpu as pltpu

F32 = jnp.float32
BF16 = jnp.bfloat16

LANE = 128
SUBLANES = 8
HALF_LANE = LANE // 2
VMEM_LIMIT_BYTES = 56 * 1024 * 1024

D_MODEL = 1024
HEAD_DIM = 64
N_HEADS = 8
N_KV_HEADS = 2
GQ = N_HEADS // N_KV_HEADS
Q_DIM = N_HEADS * HEAD_DIM
KV_DIM = N_KV_HEADS * HEAD_DIM
WINDOW = 128
BLK = 128
N_BUCKETS = 32
MAX_DISTANCE = 128
D_INNER = 1024
SSD_HEAD_DIM = 64
N_SSD_HEADS = D_INNER // SSD_HEAD_DIM
N_SSD_GROUPS = 2
D_STATE = 64
CHUNK = 128
CONV_WIDTH = 5
CONV_PAD = CONV_WIDTH // 2
BC_DIM = N_SSD_GROUPS * D_STATE
CONV_DIM = D_INNER + 2 * BC_DIM
N_EGROUPS = 4
EXPERTS_PER_GROUP = 4
N_EXPERTS = N_EGROUPS * EXPERTS_PER_GROUP
PAIRS_PER_GROUP = EXPERTS_PER_GROUP * (EXPERTS_PER_GROUP - 1) // 2
N_BUCKETS_MOE = N_EGROUPS * PAIRS_PER_GROUP
BUCKET_ROWS = 32
D_EXPERT = 256
PLE_DIM = 256
EPS = 1e-6
MASKED = -1e30
LOG2_E = 1.4426950408889634

QKV_COLS = Q_DIM + 2 * KV_DIM
GATE_COLS = 2 * D_MODEL
DT_COLS = LANE
IN_COLS_PADDED = QKV_COLS + D_INNER + CONV_DIM + GATE_COLS + DT_COLS
XBC_OFFSET = QKV_COLS + D_INNER
HALO = 16
ROUTED_WORDS = D_MODEL // 2


def _cparams(*sem):
    return pltpu.CompilerParams(dimension_semantics=sem, vmem_limit_bytes=VMEM_LIMIT_BYTES)


def _layer_block(shape, layer):
    return pl.BlockSpec((None,) + tuple(shape), lambda *_: (layer,) + (0,) * len(shape))


def _dot(a, b):
    return jnp.dot(a, b, preferred_element_type=F32)


def _dot_nt(a, b):
    return lax.dot_general(a, b, (((1,), (1,)), ((), ())), preferred_element_type=F32)


def _split_bf16(v, terms):
    parts = []
    r = v
    for _ in range(terms):
        p = r.astype(BF16)
        parts.append(p)
        r = r - p.astype(F32)
    return parts


def _rmsnorm(x, g):
    return x * lax.rsqrt(jnp.mean(x * x, axis=-1, keepdims=True) + EPS) * g


def _sigmoid(x):
    return 1.0 / (1.0 + jnp.exp(-x))


def _inproj_kernel(xp_ref, x_ref, xn_ref, g_ref, w_ref, cw_ref, cb_ref,
                   qkv_ref, z_ref, gag_ref, dt_ref, xs_ref, bc_ref, ext, *, tm, tiles_per_seq):
    g = g_ref[...]
    hb = _rmsnorm(x_ref[...], g).astype(BF16)
    h_ext = jnp.concatenate([_rmsnorm(xp_ref[...], g).astype(BF16), hb,
                             _rmsnorm(xn_ref[...], g).astype(BF16)], axis=0)
    xbc = _dot(h_ext, w_ref[:, XBC_OFFSET:XBC_OFFSET + CONV_DIM])
    pos_in_seq = pl.program_id(0) % tiles_per_seq
    first = pos_in_seq == 0
    last = pos_in_seq == tiles_per_seq - 1
    n_planes = CONV_DIM // LANE
    for cb in range(n_planes):
        cs = slice(cb * LANE, (cb + 1) * LANE)
        ext[cb, 0:HALO] = jnp.where(first, 0.0, xbc[0:HALO, cs])
        ext[cb, HALO:HALO + tm] = xbc[HALO:HALO + tm, cs]
        ext[cb, HALO + tm:] = jnp.where(last, 0.0, xbc[HALO + tm:, cs])

    def conv_plane(cb):
        cs = slice(cb * LANE, (cb + 1) * LANE)
        acc = cb_ref[:, cs] + cw_ref[0:1, cs] * ext[cb, HALO - CONV_PAD:HALO - CONV_PAD + tm]
        for k in range(1, CONV_WIDTH):
            acc = acc + cw_ref[k:k + 1, cs] * ext[cb, HALO - CONV_PAD + k:HALO - CONV_PAD + k + tm]
        y = (acc * _sigmoid(acc))
        if cb < D_INNER // LANE:
            xs_ref[:, cs] = y.astype(xs_ref.dtype)
        else:
            c0 = cb * LANE - D_INNER
            bc_ref[:, c0:c0 + LANE] = y.astype(bc_ref.dtype)

    segments = ((qkv_ref, 0), (z_ref, QKV_COLS), (gag_ref, XBC_OFFSET + CONV_DIM),
                (dt_ref, XBC_OFFSET + CONV_DIM + GATE_COLS))
    planes_after = (3, 3, 4, 0)
    cb = 0
    for (ref, off), n_conv in zip(segments, planes_after):
        ref[...] = _dot(hb, w_ref[:, off:off + ref.shape[-1]]).astype(ref.dtype)
        for _ in range(n_conv):
            conv_plane(cb)
            cb += 1
    assert cb == n_planes


def _inproj(x, g, w, conv_w, conv_b, layer, tm, seq):
    t = x.shape[0]
    r = tm // HALO
    nh = t // HALO
    widths = (QKV_COLS, D_INNER, GATE_COLS, DT_COLS, D_INNER, 2 * BC_DIM)
    dtypes = (BF16, BF16, BF16, F32, BF16, BF16)
    return pl.pallas_call(
        functools.partial(_inproj_kernel, tm=tm, tiles_per_seq=seq // tm),
        grid=(t // tm,),
        in_specs=[
            pl.BlockSpec((HALO, D_MODEL), lambda i: (jnp.maximum(i * r - 1, 0), 0)),
            pl.BlockSpec((tm, D_MODEL), lambda i: (i, 0)),
            pl.BlockSpec((HALO, D_MODEL), lambda i: (jnp.minimum((i + 1) * r, nh - 1), 0)),
            _layer_block((1, D_MODEL), layer),
            _layer_block((D_MODEL, IN_COLS_PADDED), layer),
            _layer_block((CONV_WIDTH, CONV_DIM), layer),
            _layer_block((1, CONV_DIM), layer),
        ],
        out_specs=[pl.BlockSpec((tm, n), lambda i: (i, 0)) for n in widths],
        out_shape=[jax.ShapeDtypeStruct((t, n), d) for n, d in zip(widths, dtypes)],
        scratch_shapes=[pltpu.VMEM((CONV_DIM // LANE, tm + 2 * HALO, LANE), F32)],
        compiler_params=_cparams("parallel"),
        name="inproj_conv",
    )(x, x, x, g, w, conv_w, conv_b)


STACK_HEAD_ORDER = tuple(half * GQ + p for p in range(GQ) for half in range(N_KV_HEADS))


SUBS_PER_TRIP = 4


def _attn_kernel(q_ref, kp_ref, km_ref, kn_ref, vp_ref, vm_ref, vn_ref, bias_ref, sink_ref, o_ref,
                 k_ext, v_ext, *, qb, nq):
    n = pl.program_id(1)
    k_ext[0:BLK] = kp_ref[0]
    k_ext[BLK:BLK + qb] = km_ref[0]
    k_ext[BLK + qb:] = kn_ref[0]
    v_ext[0:BLK, 0:KV_DIM] = vp_ref[0]
    v_ext[BLK:BLK + qb, 0:KV_DIM] = vm_ref[0]
    v_ext[BLK + qb:, 0:KV_DIM] = vn_ref[0]
    v_ext[:, KV_DIM:] = jnp.ones((qb + 2 * BLK, LANE), v_ext.dtype)
    low = lax.broadcasted_iota(jnp.int32, (BLK, LANE), 1) < HALF_LANE
    last_sub = qb // BLK - 1

    def sub_block(j):
        r0 = pl.multiple_of(j * BLK, BLK)
        q = q_ref[0, pl.ds(r0, BLK), :]
        zero = jnp.zeros((BLK, LANE), q.dtype)
        parts = []
        for p in range(GQ):
            qp = q[:, p * LANE:(p + 1) * LANE]
            parts += [jnp.where(low, qp, zero), jnp.where(low, zero, qp)]
        qs = jnp.concatenate(parts, axis=0)
        first = jnp.logical_and(n == 0, j == 0).astype(jnp.int32)
        last = jnp.logical_and(n == nq - 1, j == last_sub).astype(jnp.int32)
        s = _dot_nt(qs, k_ext[pl.ds(r0, 3 * BLK), :]) + bias_ref[first + 2 * last]
        sc = [s[:, c * BLK:(c + 1) * BLK] for c in range(3)]
        sk = sink_ref[...]
        row_max = jnp.max(jnp.maximum(jnp.maximum(sc[0], sc[1]), sc[2]), axis=-1, keepdims=True)
        m = jnp.maximum(row_max, sk)
        p_un = jnp.concatenate([jnp.exp(c - m) for c in sc], axis=1).astype(BF16)
        ov = _dot(p_un, v_ext[pl.ds(r0, 3 * BLK), :])
        o = ov[:, 0:KV_DIM] / (ov[:, KV_DIM:] + jnp.exp(sk - m))
        for p in range(GQ):
            o_lo = o[(2 * p) * BLK:(2 * p + 1) * BLK]
            o_hi = o[(2 * p + 1) * BLK:(2 * p + 2) * BLK]
            o_ref[0, pl.ds(r0, BLK), p * LANE:(p + 1) * LANE] = jnp.where(low, o_lo, o_hi).astype(o_ref.dtype)

    def trip(jj, carry):
        for u in range(SUBS_PER_TRIP):
            sub_block(jj * SUBS_PER_TRIP + u)
        return carry

    lax.fori_loop(0, qb // (BLK * SUBS_PER_TRIP), trip, 0)


def _attention(qkv, sink_rows, bias, layer, qb):
    b, l, _ = qkv.shape
    nq = l // qb
    r = qb // BLK
    nblk = l // BLK
    kcol, vcol = Q_DIM // LANE, Q_DIM // LANE + 1

    def prev_map(c):
        return lambda bi, n: (bi, jnp.maximum(n * r - 1, 0), c)

    def next_map(c):
        return lambda bi, n: (bi, jnp.minimum((n + 1) * r, nblk - 1), c)

    return pl.pallas_call(
        functools.partial(_attn_kernel, qb=qb, nq=nq),
        grid=(b, nq),
        in_specs=[
            pl.BlockSpec((1, qb, Q_DIM), lambda bi, n: (bi, n, 0)),
            pl.BlockSpec((1, BLK, KV_DIM), prev_map(kcol)),
            pl.BlockSpec((1, qb, KV_DIM), lambda bi, n: (bi, n, kcol)),
            pl.BlockSpec((1, BLK, KV_DIM), next_map(kcol)),
            pl.BlockSpec((1, BLK, KV_DIM), prev_map(vcol)),
            pl.BlockSpec((1, qb, KV_DIM), lambda bi, n: (bi, n, vcol)),
            pl.BlockSpec((1, BLK, KV_DIM), next_map(vcol)),
            pl.BlockSpec((4, N_HEADS * BLK, 3 * BLK), lambda bi, n: (0, 0, 0)),
            _layer_block((N_HEADS * BLK, LANE), layer),
        ],
        out_specs=pl.BlockSpec((1, qb, Q_DIM), lambda bi, n: (bi, n, 0)),
        out_shape=jax.ShapeDtypeStruct((b, l, Q_DIM), BF16),
        scratch_shapes=[pltpu.VMEM((qb + 2 * BLK, KV_DIM), BF16),
                        pltpu.VMEM((qb + 2 * BLK, KV_DIM + LANE), BF16)],
        compiler_params=_cparams("parallel", "parallel"),
        name="attention",
    )(qkv, qkv, qkv, qkv, qkv, qkv, qkv, bias, sink_rows)


def _ssd_chunk(x_ref, bc_ref, dt_ref, par_ref, e_ref, y_ref, s_ref, r0, *, fwd):
    rows = lax.broadcasted_iota(jnp.int32, (CHUNK, CHUNK), 0)
    cols = lax.broadcasted_iota(jnp.int32, (CHUNK, CHUNK), 1)
    mask = rows >= cols if fwd else rows <= cols
    low = cols < HALF_LANE
    lane0 = 0 if fwd else N_SSD_HEADS

    x = dt_ref[0, pl.ds(r0, CHUNK), :] + par_ref[0:1, :]
    dt = jnp.maximum(x, 0.0) + jnp.log(1.0 + jnp.exp(-jnp.abs(x)))
    a = dt * (-jnp.exp(par_ref[1:2, :]))
    tri = mask.astype(BF16)
    acs = sum(_dot(tri, part) for part in _split_bf16(a, 3))
    last = acs[CHUNK - 1:CHUNK, :] if fwd else acs[0:1, :]
    acs2 = acs * LOG2_E
    acs2_t = acs2.T
    dt_t = dt.T

    bc = bc_ref[0, pl.ds(r0, CHUNK), :]
    b_all = bc[:, 0:BC_DIM]
    c_all = bc[:, BC_DIM:2 * BC_DIM]
    b_t = b_all.astype(F32).T.astype(BF16)
    zero = jnp.zeros_like(c_all)
    c_grp = [jnp.where(low, c_all, zero), jnp.where(low, zero, c_all)]
    cb = [_dot_nt(c, b_all) for c in c_grp]

    def expand(v, terms):
        return sum(_dot(part, e_ref[...]) for part in _split_bf16(v, terms))

    w_state = expand(dt * jnp.exp(last - acs), 1).astype(BF16)
    w_off = expand(jnp.exp(acs), 1)
    decay = expand(jnp.broadcast_to(jnp.exp(last), (8, LANE)), 2)[0:1, :]

    xc = x_ref[0, pl.ds(r0, CHUNK), :]
    xw = xc * w_state
    x_zero = jnp.zeros((CHUNK, LANE), xc.dtype)
    for p in range(N_SSD_HEADS // 2):
        g = p // (N_SSD_HEADS // (2 * N_SSD_GROUPS))
        ps = slice(p * LANE, (p + 1) * LANE)
        xp = xc[:, ps]
        ms = []
        for half in range(2):
            lh = lane0 + 2 * p + half
            seg2 = acs2[:, lh:lh + 1] - acs2_t[lh:lh + 1, :]
            decay_mat = jnp.exp2(jnp.where(mask, seg2, MASKED))
            ms.append((cb[g] * decay_mat * dt_t[lh:lh + 1, :]).astype(BF16))
        x_heads = jnp.concatenate([jnp.where(low, xp, x_zero), jnp.where(low, x_zero, xp)], axis=0)
        y_diag = _dot(jnp.concatenate(ms, axis=1), x_heads)
        state = s_ref[p]
        y_off = _dot(c_grp[g], state.astype(BF16)) * w_off[:, ps]
        y_ref[0, pl.ds(r0, CHUNK), ps] = (y_diag + y_off).astype(y_ref.dtype)
        s_ref[p] = state * decay[:, ps] + _dot(b_t, xw[:, ps])


def _ssd_kernel(xf_ref, bcf_ref, dtf_ref, xb_ref, bcb_ref, dtb_ref, par_ref, ef_ref, eb_ref,
                yf_ref, yb_ref, sf_ref, sb_ref, *, cps):
    @pl.when(pl.program_id(1) == 0)
    def _():
        sf_ref[...] = jnp.zeros_like(sf_ref)
        sb_ref[...] = jnp.zeros_like(sb_ref)

    def body(j, carry):
        rf = pl.multiple_of(j * CHUNK, CHUNK)
        rb = pl.multiple_of((cps - 1 - j) * CHUNK, CHUNK)
        _ssd_chunk(xf_ref, bcf_ref, dtf_ref, par_ref, ef_ref, yf_ref, sf_ref, rf, fwd=True)
        _ssd_chunk(xb_ref, bcb_ref, dtb_ref, par_ref, eb_ref, yb_ref, sb_ref, rb, fwd=False)
        return carry

    lax.fori_loop(0, cps, body, 0, unroll=2)


def _ssd(xs, bc, dt, par, e_f, e_b, layer, cps):
    b, l, _ = xs.shape
    rows = cps * CHUNK
    ns = l // rows
    fmap = lambda bi, c: (bi, c, 0)
    bmap = lambda bi, c: (bi, ns - 1 - c, 0)
    const = lambda bi, c: (0, 0)
    state = pltpu.VMEM((N_SSD_HEADS // 2, LANE, LANE), F32)
    return pl.pallas_call(
        functools.partial(_ssd_kernel, cps=cps),
        grid=(b, ns),
        in_specs=[
            pl.BlockSpec((1, rows, D_INNER), fmap),
            pl.BlockSpec((1, rows, 2 * BC_DIM), fmap),
            pl.BlockSpec((1, rows, DT_COLS), fmap),
            pl.BlockSpec((1, rows, D_INNER), bmap),
            pl.BlockSpec((1, rows, 2 * BC_DIM), bmap),
            pl.BlockSpec((1, rows, DT_COLS), bmap),
            _layer_block((8, LANE), layer),
            pl.BlockSpec((LANE, D_INNER), const),
            pl.BlockSpec((LANE, D_INNER), const),
        ],
        out_specs=[pl.BlockSpec((1, rows, D_INNER), fmap), pl.BlockSpec((1, rows, D_INNER), bmap)],
        out_shape=[jax.ShapeDtypeStruct((b, l, D_INNER), BF16)] * 2,
        scratch_shapes=[state, state],
        compiler_params=_cparams("parallel", "arbitrary"),
        name="ssd_scan",
    )(xs, bc, dt, xs, bc, dt, par, e_f, e_b)


def _router_gates(lg_t):
    row = lambda j: lg_t[j:j + 1, :]
    lgrp = [row(j) for j in range(N_EGROUPS)]
    m = functools.reduce(jnp.maximum, lgrp)
    g_sel = jnp.where(lgrp[0] == m, 0, jnp.where(lgrp[1] == m, 1, jnp.where(lgrp[2] == m, 2, 3)))
    p_sel = 1.0 / sum(jnp.exp(v - m) for v in lgrp)
    s = []
    for k in range(EXPERTS_PER_GROUP):
        e = [row(N_EGROUPS + EXPERTS_PER_GROUP * g + k) for g in range(N_EGROUPS)]
        s.append(jnp.where(g_sel == 0, e[0], jnp.where(g_sel == 1, e[1], jnp.where(g_sel == 2, e[2], e[3]))))
    m1 = functools.reduce(jnp.maximum, s)
    i1 = jnp.where(s[0] == m1, 0, jnp.where(s[1] == m1, 1, jnp.where(s[2] == m1, 2, 3)))
    s2 = [jnp.where(i1 == k, -jnp.inf, s[k]) for k in range(EXPERTS_PER_GROUP)]
    m2 = functools.reduce(jnp.maximum, s2)
    i2 = jnp.where(s2[0] == m2, 0, jnp.where(s2[1] == m2, 1, jnp.where(s2[2] == m2, 2, 3)))
    t = jnp.exp(m2 - m1)
    w1 = 1.0 / (1.0 + t)
    w2 = t / (1.0 + t)
    first_is_a = i1 < i2
    a = jnp.minimum(i1, i2)
    b = jnp.maximum(i1, i2)
    gate_a = jnp.where(first_is_a, w1, w2) * p_sel
    gate_b = jnp.where(first_is_a, w2, w1) * p_sel
    pair = jnp.where(a == 0, 0, jnp.where(a == 1, 3, 5)) + b - a - 1
    bucket = g_sel * PAIRS_PER_GROUP + pair
    tm = lg_t.shape[1]
    rows = lax.broadcasted_iota(jnp.int32, (SUBLANES, tm), 0)
    gates = jnp.where(rows == 0, gate_a, jnp.where(rows == 1, gate_b, 0.0))
    return jnp.concatenate([gates, jnp.zeros((LANE - SUBLANES, tm), F32)], axis=0), bucket


def _pack_bf16_pairs(h):
    bits = lax.bitcast_convert_type(h.astype(BF16).astype(F32), jnp.uint32)
    half = h.shape[1] // 2
    return (bits[:, :half] >> 16) | (bits[:, half:] & jnp.uint32(0xFFFF0000))


def _unpack_bf16_pairs(w):
    lo = lax.bitcast_convert_type(w << 16, F32).astype(BF16)
    hi = lax.bitcast_convert_type(w & jnp.uint32(0xFFFF0000), F32).astype(BF16)
    return lo, hi


def _merge_kernel(yf_ref, yb_ref, xs_ref, z_ref, att_ref, gag_ref, x_ref, dsk_ref, ng_ref,
                  wso_ref, wao_ref, wo_ref, gf_ref, wrh_ref, wrhl_ref, rb_ref, upper_ref,
                  x1_ref, h2g_ref, grp_ref, rank_ref, counts_ref, cnt):
    @pl.when(pl.program_id(0) == 0)
    def _():
        cnt[...] = jnp.zeros_like(cnt)

    y = yf_ref[...].astype(F32) + yb_ref[...].astype(F32) + dsk_ref[...] * xs_ref[...].astype(F32)
    z = z_ref[...].astype(F32)
    y = y * (z * _sigmoid(z))
    gsz = D_INNER // N_SSD_GROUPS
    parts = []
    for g in range(N_SSD_GROUPS):
        yg = y[:, g * gsz:(g + 1) * gsz]
        parts.append(yg * lax.rsqrt(jnp.mean(yg * yg, axis=-1, keepdims=True) + EPS))
    ysn = (jnp.concatenate(parts, axis=1) * ng_ref[...]).astype(BF16)
    ys = _dot(ysn, wso_ref[...])
    ya = _dot(att_ref[...], wao_ref[...])
    ga = gag_ref[:, 0:D_MODEL].astype(F32)
    gs = gag_ref[:, D_MODEL:2 * D_MODEL].astype(F32)
    mixed = _sigmoid(ga) * ya + _sigmoid(gs) * ys
    x1 = x_ref[...] + _dot(mixed.astype(BF16), wo_ref[...])
    x1_ref[...] = x1
    h2 = _rmsnorm(x1, gf_ref[...])
    h2g_ref[:, 0:ROUTED_WORDS] = _pack_bf16_pairs(h2)
    hh, hl = _split_bf16(h2, 2)
    hi_terms = _dot(hh, wrhl_ref[...])
    lg = hi_terms[:, 0:LANE] + _dot(hl, wrh_ref[...]) + hi_terms[:, LANE:] + rb_ref[...]
    tm = x1.shape[0]
    gates_t, bucket = _router_gates(lg.T)
    h2g_ref[:, ROUTED_WORDS:] = lax.bitcast_convert_type(gates_t.T, jnp.uint32)
    onehot = (lax.broadcasted_iota(jnp.int32, (BUCKET_ROWS, tm), 0) == bucket).astype(F32)
    rank_all = _dot(onehot.astype(BF16), upper_ref[...]) + cnt[:, 0:1]
    rank_ref[0] = jnp.sum(onehot * rank_all, axis=0, keepdims=True).astype(jnp.int32)
    grp_ref[0] = bucket
    cnt[...] += jnp.sum(onehot, axis=1, keepdims=True)
    counts_ref[...] = cnt[...]


def _merge(yf, yb, xs, z, att, gag, x, dsk, ng, wso, wao, wo, gf, wrh, wrhl, rb, layer, tm):
    t = x.shape[0]
    nt = t // tm
    upper = jnp.asarray(np.triu(np.ones((tm, tm), np.float32), 1), BF16)
    tile = lambda n: pl.BlockSpec((tm, n), lambda i: (i, 0))
    per_layer = lambda a: _layer_block(a.shape[1:], layer)
    row = pl.BlockSpec((1, 1, tm), lambda i: (i, 0, 0))
    consts = (dsk, ng, wso, wao, wo, gf, wrh, wrhl, rb)
    return pl.pallas_call(
        _merge_kernel,
        grid=(nt,),
        in_specs=[tile(D_INNER), tile(D_INNER), tile(D_INNER), tile(D_INNER), tile(Q_DIM),
                  tile(GATE_COLS), tile(D_MODEL)] + [per_layer(a) for a in consts]
                 + [pl.BlockSpec((tm, tm), lambda i: (0, 0))],
        out_specs=[tile(D_MODEL), tile(ROUTED_WORDS + LANE), row, row, pl.BlockSpec((BUCKET_ROWS, LANE), lambda i: (0, 0))],
        out_shape=[jax.ShapeDtypeStruct((t, D_MODEL), F32),
                   jax.ShapeDtypeStruct((t, ROUTED_WORDS + LANE), jnp.uint32),
                   jax.ShapeDtypeStruct((nt, 1, tm), jnp.int32),
                   jax.ShapeDtypeStruct((nt, 1, tm), jnp.int32),
                   jax.ShapeDtypeStruct((BUCKET_ROWS, LANE), F32)],
        scratch_shapes=[pltpu.VMEM((BUCKET_ROWS, LANE), F32)],
        compiler_params=_cparams("arbitrary"),
        name="merge_router",
    )(yf, yb, xs, z, att, gag, x, *consts, upper)


def _dma_params(*sem):
    return pltpu.CompilerParams(dimension_semantics=sem, vmem_limit_bytes=VMEM_LIMIT_BYTES,
                                disable_bounds_checks=True)


def _permute_kernel(pos_ref, src_ref, init_ref, dst_ref, stage, sem, *, rows, nt):
    del init_ref
    i = pl.program_id(0)
    slot = i % 2
    stage[slot] = src_ref[...]

    def issue(r8, carry):
        base = pl.multiple_of(r8 * SUBLANES, SUBLANES)
        src_tile = stage.at[slot, pl.ds(base, SUBLANES)]
        for k in range(SUBLANES):
            pltpu.make_async_copy(src_tile.at[pl.ds(k, 1)], dst_ref.at[pl.ds(pos_ref[0, 0, base + k], 1)],
                                  sem.at[slot]).start()
        return carry

    lax.fori_loop(0, rows // SUBLANES, issue, 0)

    def wait_all(s):
        pltpu.make_async_copy(stage.at[s], dst_ref.at[pl.ds(0, rows)], sem.at[s]).wait()

    @pl.when(i > 0)
    def _():
        wait_all(1 - slot)

    @pl.when(i == nt - 1)
    def _():
        wait_all(slot)


def _permute(h2g, pos, t_pad, rows):
    t, w = h2g.shape
    nt = t // rows
    return pl.pallas_call(
        functools.partial(_permute_kernel, rows=rows, nt=nt),
        grid=(nt,),
        in_specs=[pl.BlockSpec((1, 1, rows), lambda i: (i, 0, 0), memory_space=pltpu.SMEM),
                  pl.BlockSpec((rows, w), lambda i: (i, 0)),
                  pl.BlockSpec(memory_space=pl.ANY)],
        out_specs=pl.BlockSpec(memory_space=pl.ANY),
        out_shape=jax.ShapeDtypeStruct((t_pad, w), h2g.dtype),
        scratch_shapes=[pltpu.VMEM((2, rows, w), h2g.dtype), pltpu.SemaphoreType.DMA((2,))],
        input_output_aliases={2: 0},
        compiler_params=_dma_params("arbitrary"),
        name="moe_permute",
    )(pos.reshape(nt, 1, rows), h2g, jnp.zeros((t_pad, w), h2g.dtype))


def _experts_kernel(ea_ref, eb_ref, tv_ref, hs_ref, wgu_a, wd_a, wgu_b, wd_b, o_ref):
    del ea_ref, eb_ref
    i = pl.program_id(0)

    @pl.when(tv_ref[i] == 0)
    def _():
        o_ref[...] = jnp.zeros_like(o_ref)

    @pl.when(tv_ref[i] != 0)
    def _():
        lo, hi = _unpack_bf16_pairs(hs_ref[:, 0:ROUTED_WORDS])
        gates = lax.bitcast_convert_type(hs_ref[:, ROUTED_WORDS:], F32)
        acc = jnp.zeros(o_ref.shape, F32)
        for lane, (wgu, wd) in enumerate(((wgu_a, wd_a), (wgu_b, wd_b))):
            gu = _dot(lo, wgu[0, 0:ROUTED_WORDS, :]) + _dot(hi, wgu[0, ROUTED_WORDS:, :])
            g = gu[:, 0:D_EXPERT]
            u = gu[:, D_EXPERT:2 * D_EXPERT]
            a = (g * _sigmoid(g)) * u * gates[:, lane:lane + 1]
            acc = acc + _dot(a.astype(BF16), wd[0])
        o_ref[...] = acc


def _experts(hs, expert_a, expert_b, tile_valid, wgu, wd, layer, tms):
    t_pad, w = hs.shape
    pick_a = lambda i, ea, eb, tv: (layer, ea[i], 0, 0)
    pick_b = lambda i, ea, eb, tv: (layer, eb[i], 0, 0)
    return pl.pallas_call(
        _experts_kernel,
        grid_spec=pltpu.PrefetchScalarGridSpec(
            num_scalar_prefetch=3,
            grid=(t_pad // tms,),
            in_specs=[
                pl.BlockSpec((tms, w), lambda i, ea, eb, tv: (i, 0)),
                pl.BlockSpec((None, 1, D_MODEL, 2 * D_EXPERT), pick_a),
                pl.BlockSpec((None, 1, D_EXPERT, D_MODEL), pick_a),
                pl.BlockSpec((None, 1, D_MODEL, 2 * D_EXPERT), pick_b),
                pl.BlockSpec((None, 1, D_EXPERT, D_MODEL), pick_b),
            ],
            out_specs=pl.BlockSpec((tms, D_MODEL), lambda i, ea, eb, tv: (i, 0)),
        ),
        out_shape=jax.ShapeDtypeStruct((t_pad, D_MODEL), F32),
        compiler_params=_cparams("arbitrary"),
        name="moe_experts",
    )(expert_a, expert_b, tile_valid, hs, wgu, wd, wgu, wd)


def _ple_kernel(pos_ref, posn_ref, x_ref, p_ref, g_ref, wg_ref, wp_ref, gfin_ref, ys_ref, o_ref, buf, sem,
                *, tm, nt, final):
    i = pl.program_id(0)
    slot = i % 2

    def gather(pref, s):
        def issue(r8, carry):
            base = pl.multiple_of(r8 * SUBLANES, SUBLANES)
            dst_tile = buf.at[s, pl.ds(base, SUBLANES)]
            for k in range(SUBLANES):
                pltpu.make_async_copy(ys_ref.at[pl.ds(pref[0, 0, base + k], 1)], dst_tile.at[pl.ds(k, 1)],
                                      sem.at[s]).start()
            return carry
        lax.fori_loop(0, tm // SUBLANES, issue, 0)

    @pl.when(i == 0)
    def _():
        gather(pos_ref, 0)

    @pl.when(i + 1 < nt)
    def _():
        gather(posn_ref, 1 - slot)

    pltpu.make_async_copy(ys_ref.at[pl.ds(0, tm)], buf.at[slot], sem.at[slot]).wait()
    x = x_ref[...] + buf[slot]
    gate = _sigmoid(_dot(_rmsnorm(x, g_ref[...]).astype(BF16), wg_ref[...]))
    y = x + gate * _dot(p_ref[...].astype(BF16), wp_ref[...])
    o_ref[...] = _rmsnorm(y, gfin_ref[...]) if final else y


def _ple(x, ys, pos, p, g, wg, wp, g_final, layer, final, tm):
    t = x.shape[0]
    nt = t // tm
    pos3 = pos.reshape(nt, 1, tm)
    return pl.pallas_call(
        functools.partial(_ple_kernel, tm=tm, nt=nt, final=final),
        grid=(nt,),
        in_specs=[
            pl.BlockSpec((1, 1, tm), lambda i: (i, 0, 0), memory_space=pltpu.SMEM),
            pl.BlockSpec((1, 1, tm), lambda i: (jnp.minimum(i + 1, nt - 1), 0, 0), memory_space=pltpu.SMEM),
            pl.BlockSpec((tm, D_MODEL), lambda i: (i, 0)),
            pl.BlockSpec((None, tm, PLE_DIM), lambda i: (layer, i, 0)),
            _layer_block((1, D_MODEL), layer),
            _layer_block((D_MODEL, D_MODEL), layer),
            _layer_block((PLE_DIM, D_MODEL), layer),
            pl.BlockSpec((1, D_MODEL), lambda i: (0, 0)),
            pl.BlockSpec(memory_space=pl.ANY),
        ],
        out_specs=pl.BlockSpec((tm, D_MODEL), lambda i: (i, 0)),
        out_shape=jax.ShapeDtypeStruct((t, D_MODEL), F32),
        scratch_shapes=[pltpu.VMEM((2, tm, D_MODEL), F32), pltpu.SemaphoreType.DMA((2,))],
        compiler_params=_dma_params("arbitrary"),
        name="ple_gate",
    )(pos3, pos3, x, p, g, wg, wp, g_final, ys)


def _route(bucket, rank, counts, t, tms):
    cnt = counts[:N_BUCKETS_MOE, 0].astype(jnp.int32)
    padded = (cnt + tms - 1) // tms * tms
    ends = jnp.cumsum(padded)
    starts = ends - padded
    bucket = bucket.reshape(t)
    pos = rank.reshape(t) + sum(jnp.where(bucket == k, starts[k], 0) for k in range(N_BUCKETS_MOE))
    t_pad = t + N_BUCKETS_MOE * tms
    tile_start = jnp.arange(t_pad // tms, dtype=jnp.int32) * tms
    tile_bucket = jnp.minimum(sum((tile_start >= ends[k]).astype(jnp.int32) for k in range(N_BUCKETS_MOE)),
                              N_BUCKETS_MOE - 1)
    tile_valid = (tile_start < ends[N_BUCKETS_MOE - 1]).astype(jnp.int32)
    pairs = [(a, b) for a in range(EXPERTS_PER_GROUP) for b in range(a + 1, EXPERTS_PER_GROUP)]
    group, pair = tile_bucket // PAIRS_PER_GROUP, tile_bucket % PAIRS_PER_GROUP
    first = sum(jnp.where(pair == k, a, 0) for k, (a, _) in enumerate(pairs))
    second = sum(jnp.where(pair == k, b, 0) for k, (_, b) in enumerate(pairs))
    expert_a = (group * EXPERTS_PER_GROUP + first).astype(jnp.int32)
    expert_b = (group * EXPERTS_PER_GROUP + second).astype(jnp.int32)
    return pos, expert_a, expert_b, tile_valid, t_pad


def _band_bias(rel_bias):
    rel = (np.arange(3 * BLK)[None, :] - BLK) - np.arange(BLK)[:, None]
    nb = N_BUCKETS // 2
    max_exact = nb // 2
    ret = (rel > 0).astype(np.int32) * nb
    n = np.abs(rel)
    large = max_exact + (np.log(np.maximum(n, 1) / max_exact) / np.log(MAX_DISTANCE / max_exact)
                         * (nb - max_exact)).astype(np.int32)
    large = np.minimum(large, nb - 1)
    bucket = (ret + np.where(n < max_exact, n, large)).astype(np.int32)
    table = rel_bias.astype(F32)
    bucket = jnp.asarray(bucket)[None]
    bias = jnp.zeros((N_HEADS, BLK, 3 * BLK), F32)
    for k in range(N_BUCKETS):
        bias = jnp.where(bucket == k, table[k][:, None, None], bias)
    bias = jnp.where(jnp.asarray(n <= WINDOW)[None], bias, MASKED)
    bias = bias[np.asarray(STACK_HEAD_ORDER)].reshape(N_HEADS * BLK, 3 * BLK)
    key_blk = jnp.arange(3 * BLK) // BLK
    no_left = jnp.where(key_blk == 0, MASKED, bias)
    no_right = jnp.where(key_blk == 2, MASKED, bias)
    return jnp.stack([bias, no_left, no_right, jnp.where(key_blk == 2, MASKED, no_left)])


def _by_head(a, axis):
    return jnp.concatenate([lax.slice_in_dim(a, h * HEAD_DIM, (h + 1) * HEAD_DIM, axis=axis)
                            for h in STACK_HEAD_ORDER], axis=axis)


def _head_expand_matrix(first_lane):
    e = np.zeros((LANE, D_INNER), np.float32)
    for h in range(N_SSD_HEADS):
        e[first_lane + h, h * SSD_HEAD_DIM:(h + 1) * SSD_HEAD_DIM] = 1.0
    return jnp.asarray(e, BF16)


def _prep_layers(P):
    depth = P["w_in"].shape[0]
    w = P["w_in"]
    o = np.cumsum([0, Q_DIM, KV_DIM, KV_DIM, D_INNER, CONV_DIM, N_SSD_HEADS, N_SSD_HEADS, D_MODEL, D_MODEL])
    q, k, v, z, xbc, dtf, dtb, ga, gs = [w[:, :, o[i]:o[i + 1]] for i in range(9)]
    dt_pad = jnp.zeros((depth, D_MODEL, DT_COLS - 2 * N_SSD_HEADS), F32)
    w_in = jnp.concatenate([_by_head(q, 2) * (HEAD_DIM ** -0.5), k, v, z, xbc, ga, gs, dtf, dtb, dt_pad],
                           axis=-1).astype(BF16)

    pad_lanes = lambda a: jnp.pad(a, ((0, 0), (0, LANE - a.shape[-1])))
    par = jnp.stack([pad_lanes(jnp.concatenate([P["dt_bias_f"], P["dt_bias_b"]], axis=-1)),
                     pad_lanes(jnp.concatenate([P["a_log_f"], P["a_log_b"]], axis=-1))], axis=1)
    par = jnp.pad(par.astype(F32), ((0, 0), (0, 6), (0, 0)))

    wr = jnp.concatenate([P["router_g"], P["router_e"]], axis=-1)
    wr = jnp.pad(wr, ((0, 0), (0, 0), (0, LANE - wr.shape[-1])))
    wr_hi = wr.astype(BF16)
    wr_hilo = jnp.concatenate([wr_hi, (wr - wr_hi.astype(F32)).astype(BF16)], axis=-1)
    rb = pad_lanes(jnp.concatenate([P["router_g_b"], P["router_e_b"]], axis=-1))[:, None, :]

    row = lambda a: a.astype(F32)[:, None, :]
    return dict(
        g_mix=row(P["g_mix"]), w_in=w_in, sink=jnp.broadcast_to(jnp.repeat(P["attn_sink"].astype(F32)[:, np.asarray(STACK_HEAD_ORDER)], BLK, axis=1)[:, :, None],
                              (depth, N_HEADS * BLK, LANE)),
        w_attn_out=_by_head(P["w_attn_out"], 1).astype(BF16),
        conv_w=P["conv_w"].astype(F32), conv_b=row(P["conv_b"]), par=par,
        d_skip=row(jnp.repeat(P["d_skip"], SSD_HEAD_DIM, axis=-1)), ssd_norm_g=row(P["ssd_norm_g"]),
        w_ssd_out=P["w_ssd_out"].astype(BF16), w_out=P["w_out"].astype(BF16),
        g_ffn=row(P["g_ffn"]), wr_hi=wr_hi, wr_hilo=wr_hilo, rb=rb.astype(F32),
        w_gate_up=P["w_gate_up"].astype(BF16), w_down=P["w_down"].astype(BF16),
        g_ple=row(P["g_ple"]), w_ple_gate=P["w_ple_gate"].astype(BF16),
        w_ple_proj=P["w_ple_proj"].astype(BF16),
    )


def _tile(n, want):
    t = min(n, want)
    assert n % t == 0
    return t


def _trunk(x, p, layers, bias, e_f, e_b, g_final):
    b, l, _ = x.shape
    t = b * l
    tm = _tile(t, 512)
    seq_tile = _tile(l, 512)
    cps = seq_tile // CHUNK

    W = layers
    depth = p.shape[0]
    p = p.reshape(depth, t, PLE_DIM)
    g_final = g_final.astype(F32)[None, :]
    xf = x.reshape(t, D_MODEL)
    for i in range(depth):
        qkv, z, gag, dt, xs, bc = _inproj(xf, W["g_mix"], W["w_in"], W["conv_w"], W["conv_b"], i, tm, l)
        att = _attention(qkv.reshape(b, l, QKV_COLS), W["sink"], bias, i, seq_tile)
        yf, yb = _ssd(xs.reshape(b, l, D_INNER), bc.reshape(b, l, 2 * BC_DIM), dt.reshape(b, l, DT_COLS),
                      W["par"], e_f, e_b, i, cps)
        x1, h2g, grp, rank, counts = _merge(
            yf.reshape(t, D_INNER), yb.reshape(t, D_INNER), xs.reshape(t, D_INNER), z, att.reshape(t, Q_DIM), gag,
            xf, W["d_skip"], W["ssd_norm_g"], W["w_ssd_out"], W["w_attn_out"], W["w_out"], W["g_ffn"],
            W["wr_hi"], W["wr_hilo"], W["rb"], i, tm)
        pos, expert_a, expert_b, tile_valid, t_pad = _route(grp, rank, counts, t, tm)
        hs = _permute(h2g, pos, t_pad, tm)
        ys = _experts(hs, expert_a, expert_b, tile_valid, W["w_gate_up"], W["w_down"], i, tm)
        xf = _ple(x1, ys, pos, p, W["g_ple"], W["w_ple_gate"], W["w_ple_proj"], g_final, i, i == depth - 1, tm)
    return xf.reshape(b, l, D_MODEL)


def kernel(x_prompt, x_sample, p_prompt, p_sample, g_mix, w_in, rel_bias, attn_sink, w_attn_out, conv_w, conv_b, a_log_f, a_log_b, dt_bias_f, dt_bias_b, d_skip, ssd_norm_g, w_ssd_out, w_out, g_ffn, router_g, router_g_b, router_e, router_e_b, w_gate_up, w_down, g_ple, w_ple_gate, w_ple_proj, g_final):
    P = dict(g_mix=g_mix, w_in=w_in, attn_sink=attn_sink, w_attn_out=w_attn_out, conv_w=conv_w, conv_b=conv_b,
             a_log_f=a_log_f, a_log_b=a_log_b, dt_bias_f=dt_bias_f, dt_bias_b=dt_bias_b, d_skip=d_skip,
             ssd_norm_g=ssd_norm_g, w_ssd_out=w_ssd_out, w_out=w_out, g_ffn=g_ffn, router_g=router_g,
             router_g_b=router_g_b, router_e=router_e, router_e_b=router_e_b, w_gate_up=w_gate_up,
             w_down=w_down, g_ple=g_ple, w_ple_gate=w_ple_gate, w_ple_proj=w_ple_proj)
    layers = _prep_layers(P)
    bias = _band_bias(rel_bias)
    e_f = _head_expand_matrix(0)
    e_b = _head_expand_matrix(N_SSD_HEADS)
    y_prompt = _trunk(x_prompt, p_prompt, layers, bias, e_f, e_b, g_final)
    y_sample = _trunk(x_sample, p_sample, layers, bias, e_f, e_b, g_final)
    return (y_prompt, y_sample)
```

```python
import functools

import numpy as np
import jax
import jax.numpy as jnp
from jax import lax
from jax.experimental import pallas as pl
from jax.experimental.pallas import tpu as pltpu

F32 = jnp.float32
BF16 = jnp.bfloat16

LANE = 128
SUBLANES = 8
HALF_LANE = LANE // 2
VMEM_LIMIT_BYTES = 56 * 1024 * 1024

D_MODEL = 1024
HEAD_DIM = 64
N_HEADS = 8
N_KV_HEADS = 2
GQ = N_HEADS // N_KV_HEADS
Q_DIM = N_HEADS * HEAD_DIM
KV_DIM = N_KV_HEADS * HEAD_DIM
WINDOW = 128
BLK = 128
N_BUCKETS = 32
MAX_DISTANCE = 128
D_INNER = 1024
SSD_HEAD_DIM = 64
N_SSD_HEADS = D_INNER // SSD_HEAD_DIM
N_SSD_GROUPS = 2
D_STATE = 64
CHUNK = 128
CONV_WIDTH = 5
CONV_PAD = CONV_WIDTH // 2
BC_DIM = N_SSD_GROUPS * D_STATE
CONV_DIM = D_INNER + 2 * BC_DIM
N_EGROUPS = 4
EXPERTS_PER_GROUP = 4
N_EXPERTS = N_EGROUPS * EXPERTS_PER_GROUP
PAIRS_PER_GROUP = EXPERTS_PER_GROUP * (EXPERTS_PER_GROUP - 1) // 2
N_BUCKETS_MOE = N_EGROUPS * PAIRS_PER_GROUP
BUCKET_ROWS = 32
D_EXPERT = 256
PLE_DIM = 256
EPS = 1e-6
MASKED = -1e30
LOG2_E = 1.4426950408889634

QKV_COLS = Q_DIM + 2 * KV_DIM
GATE_COLS = 2 * D_MODEL
DT_COLS = LANE
IN_COLS_PADDED = QKV_COLS + D_INNER + CONV_DIM + GATE_COLS + DT_COLS
XBC_OFFSET = QKV_COLS + D_INNER
HALO = 16
ROUTED_WORDS = D_MODEL // 2


def _cparams(*sem):
    return pltpu.CompilerParams(dimension_semantics=sem, vmem_limit_bytes=VMEM_LIMIT_BYTES)


def _layer_block(shape, layer):
    return pl.BlockSpec((None,) + tuple(shape), lambda *_: (layer,) + (0,) * len(shape))


def _dot(a, b):
    return jnp.dot(a, b, preferred_element_type=F32)


def _dot_nt(a, b):
    return lax.dot_general(a, b, (((1,), (1,)), ((), ())), preferred_element_type=F32)


def _split_bf16(v, terms):
    parts = []
    r = v
    for _ in range(terms):
        p = r.astype(BF16)
        parts.append(p)
        r = r - p.astype(F32)
    return parts


def _rmsnorm(x, g):
    return x * lax.rsqrt(jnp.mean(x * x, axis=-1, keepdims=True) + EPS) * g


def _sigmoid(x):
    return 1.0 / (1.0 + jnp.exp(-x))


def _inproj_kernel(xp_ref, x_ref, xn_ref, g_ref, w_ref, cw_ref, cb_ref,
                   qkv_ref, z_ref, gag_ref, dt_ref, xs_ref, bc_ref, ext, *, tm, tiles_per_seq):
    g = g_ref[...]
    hb = _rmsnorm(x_ref[...], g).astype(BF16)
    h_ext = jnp.concatenate([_rmsnorm(xp_ref[...], g).astype(BF16), hb,
                             _rmsnorm(xn_ref[...], g).astype(BF16)], axis=0)
    xbc = _dot(h_ext, w_ref[:, XBC_OFFSET:XBC_OFFSET + CONV_DIM])
    pos_in_seq = pl.program_id(0) % tiles_per_seq
    first = pos_in_seq == 0
    last = pos_in_seq == tiles_per_seq - 1
    n_planes = CONV_DIM // LANE
    for cb in range(n_planes):
        cs = slice(cb * LANE, (cb + 1) * LANE)
        ext[cb, 0:HALO] = jnp.where(first, 0.0, xbc[0:HALO, cs])
        ext[cb, HALO:HALO + tm] = xbc[HALO:HALO + tm, cs]
        ext[cb, HALO + tm:] = jnp.where(last, 0.0, xbc[HALO + tm:, cs])

    def conv_plane(cb):
        cs = slice(cb * LANE, (cb + 1) * LANE)
        acc = cb_ref[:, cs] + cw_ref[0:1, cs] * ext[cb, HALO - CONV_PAD:HALO - CONV_PAD + tm]
        for k in range(1, CONV_WIDTH):
            acc = acc + cw_ref[k:k + 1, cs] * ext[cb, HALO - CONV_PAD + k:HALO - CONV_PAD + k + tm]
        y = (acc * _sigmoid(acc))
        if cb < D_INNER // LANE:
            xs_ref[:, cs] = y.astype(xs_ref.dtype)
        else:
            c0 = cb * LANE - D_INNER
            bc_ref[:, c0:c0 + LANE] = y.astype(bc_ref.dtype)

    segments = ((qkv_ref, 0), (z_ref, QKV_COLS), (gag_ref, XBC_OFFSET + CONV_DIM),
                (dt_ref, XBC_OFFSET + CONV_DIM + GATE_COLS))
    planes_after = (3, 3, 4, 0)
    cb = 0
    for (ref, off), n_conv in zip(segments, planes_after):
        ref[...] = _dot(hb, w_ref[:, off:off + ref.shape[-1]]).astype(ref.dtype)
        for _ in range(n_conv):
            conv_plane(cb)
            cb += 1
    assert cb == n_planes


def _inproj(x, g, w, conv_w, conv_b, layer, tm, seq):
    t = x.shape[0]
    r = tm // HALO
    nh = t // HALO
    widths = (QKV_COLS, D_INNER, GATE_COLS, DT_COLS, D_INNER, 2 * BC_DIM)
    dtypes = (BF16, BF16, BF16, F32, BF16, BF16)
    return pl.pallas_call(
        functools.partial(_inproj_kernel, tm=tm, tiles_per_seq=seq // tm),
        grid=(t // tm,),
        in_specs=[
            pl.BlockSpec((HALO, D_MODEL), lambda i: (jnp.maximum(i * r - 1, 0), 0)),
            pl.BlockSpec((tm, D_MODEL), lambda i: (i, 0)),
            pl.BlockSpec((HALO, D_MODEL), lambda i: (jnp.minimum((i + 1) * r, nh - 1), 0)),
            _layer_block((1, D_MODEL), layer),
            _layer_block((D_MODEL, IN_COLS_PADDED), layer),
            _layer_block((CONV_WIDTH, CONV_DIM), layer),
            _layer_block((1, CONV_DIM), layer),
        ],
        out_specs=[pl.BlockSpec((tm, n), lambda i: (i, 0)) for n in widths],
        out_shape=[jax.ShapeDtypeStruct((t, n), d) for n, d in zip(widths, dtypes)],
        scratch_shapes=[pltpu.VMEM((CONV_DIM // LANE, tm + 2 * HALO, LANE), F32)],
        compiler_params=_cparams("parallel"),
        name="inproj_conv",
    )(x, x, x, g, w, conv_w, conv_b)


STACK_HEAD_ORDER = tuple(half * GQ + p for p in range(GQ) for half in range(N_KV_HEADS))


SUBS_PER_TRIP = 4


def _attn_kernel(q_ref, kp_ref, km_ref, kn_ref, vp_ref, vm_ref, vn_ref, bias_ref, sink_ref, o_ref,
                 k_ext, v_ext, *, qb, nq):
    n = pl.program_id(1)
    k_ext[0:BLK] = kp_ref[0]
    k_ext[BLK:BLK + qb] = km_ref[0]
    k_ext[BLK + qb:] = kn_ref[0]
    v_ext[0:BLK, 0:KV_DIM] = vp_ref[0]
    v_ext[BLK:BLK + qb, 0:KV_DIM] = vm_ref[0]
    v_ext[BLK + qb:, 0:KV_DIM] = vn_ref[0]
    v_ext[:, KV_DIM:] = jnp.ones((qb + 2 * BLK, LANE), v_ext.dtype)
    low = lax.broadcasted_iota(jnp.int32, (BLK, LANE), 1) < HALF_LANE
    last_sub = qb // BLK - 1

    def sub_block(j):
        r0 = pl.multiple_of(j * BLK, BLK)
        q = q_ref[0, pl.ds(r0, BLK), :]
        zero = jnp.zeros((BLK, LANE), q.dtype)
        parts = []
        for p in range(GQ):
            qp = q[:, p * LANE:(p + 1) * LANE]
            parts += [jnp.where(low, qp, zero), jnp.where(low, zero, qp)]
        qs = jnp.concatenate(parts, axis=0)
        first = jnp.logical_and(n == 0, j == 0).astype(jnp.int32)
        last = jnp.logical_and(n == nq - 1, j == last_sub).astype(jnp.int32)
        s = _dot_nt(qs, k_ext[pl.ds(r0, 3 * BLK), :]) + bias_ref[first + 2 * last]
        sc = [s[:, c * BLK:(c + 1) * BLK] for c in range(3)]
        sk = sink_ref[...]
        row_max = jnp.max(jnp.maximum(jnp.maximum(sc[0], sc[1]), sc[2]), axis=-1, keepdims=True)
        m = jnp.maximum(row_max, sk)
        p_un = jnp.concatenate([jnp.exp(c - m) for c in sc], axis=1).astype(BF16)
        ov = _dot(p_un, v_ext[pl.ds(r0, 3 * BLK), :])
        o = ov[:, 0:KV_DIM] / (ov[:, KV_DIM:] + jnp.exp(sk - m))
        for p in range(GQ):
            o_lo = o[(2 * p) * BLK:(2 * p + 1) * BLK]
            o_hi = o[(2 * p + 1) * BLK:(2 * p + 2) * BLK]
            o_ref[0, pl.ds(r0, BLK), p * LANE:(p + 1) * LANE] = jnp.where(low, o_lo, o_hi).astype(o_ref.dtype)

    def trip(jj, carry):
        for u in range(SUBS_PER_TRIP):
            sub_block(jj * SUBS_PER_TRIP + u)
        return carry

    lax.fori_loop(0, qb // (BLK * SUBS_PER_TRIP), trip, 0)


def _attention(qkv, sink_rows, bias, layer, qb):
    b, l, _ = qkv.shape
    nq = l // qb
    r = qb // BLK
    nblk = l // BLK
    kcol, vcol = Q_DIM // LANE, Q_DIM // LANE + 1

    def prev_map(c):
        return lambda bi, n: (bi, jnp.maximum(n * r - 1, 0), c)

    def next_map(c):
        return lambda bi, n: (bi, jnp.minimum((n + 1) * r, nblk - 1), c)

    return pl.pallas_call(
        functools.partial(_attn_kernel, qb=qb, nq=nq),
        grid=(b, nq),
        in_specs=[
            pl.BlockSpec((1, qb, Q_DIM), lambda bi, n: (bi, n, 0)),
            pl.BlockSpec((1, BLK, KV_DIM), prev_map(kcol)),
            pl.BlockSpec((1, qb, KV_DIM), lambda bi, n: (bi, n, kcol)),
            pl.BlockSpec((1, BLK, KV_DIM), next_map(kcol)),
            pl.BlockSpec((1, BLK, KV_DIM), prev_map(vcol)),
            pl.BlockSpec((1, qb, KV_DIM), lambda bi, n: (bi, n, vcol)),
            pl.BlockSpec((1, BLK, KV_DIM), next_map(vcol)),
            pl.BlockSpec((4, N_HEADS * BLK, 3 * BLK), lambda bi, n: (0, 0, 0)),
            _layer_block((N_HEADS * BLK, LANE), layer),
        ],
        out_specs=pl.BlockSpec((1, qb, Q_DIM), lambda bi, n: (bi, n, 0)),
        out_shape=jax.ShapeDtypeStruct((b, l, Q_DIM), BF16),
        scratch_shapes=[pltpu.VMEM((qb + 2 * BLK, KV_DIM), BF16),
                        pltpu.VMEM((qb + 2 * BLK, KV_DIM + LANE), BF16)],
        compiler_params=_cparams("parallel", "parallel"),
        name="attention",
    )(qkv, qkv, qkv, qkv, qkv, qkv, qkv, bias, sink_rows)


def _ssd_chunk(x_ref, bc_ref, dt_ref, par_ref, e_ref, y_ref, s_ref, r0, *, fwd):
    rows = lax.broadcasted_iota(jnp.int32, (CHUNK, CHUNK), 0)
    cols = lax.broadcasted_iota(jnp.int32, (CHUNK, CHUNK), 1)
    mask = rows >= cols if fwd else rows <= cols
    low = cols < HALF_LANE
    lane0 = 0 if fwd else N_SSD_HEADS

    x = dt_ref[0, pl.ds(r0, CHUNK), :] + par_ref[0:1, :]
    dt = jnp.maximum(x, 0.0) + jnp.log(1.0 + jnp.exp(-jnp.abs(x)))
    a = dt * (-jnp.exp(par_ref[1:2, :]))
    tri = mask.astype(BF16)
    acs = sum(_dot(tri, part) for part in _split_bf16(a, 3))
    last = acs[CHUNK - 1:CHUNK, :] if fwd else acs[0:1, :]
    acs2 = acs * LOG2_E
    acs2_t = acs2.T
    dt_t = dt.T

    bc = bc_ref[0, pl.ds(r0, CHUNK), :]
    b_all = bc[:, 0:BC_DIM]
    c_all = bc[:, BC_DIM:2 * BC_DIM]
    b_t = b_all.astype(F32).T.astype(BF16)
    zero = jnp.zeros_like(c_all)
    c_grp = [jnp.where(low, c_all, zero), jnp.where(low, zero, c_all)]
    cb = [_dot_nt(c, b_all) for c in c_grp]

    def expand(v, terms):
        return sum(_dot(part, e_ref[...]) for part in _split_bf16(v, terms))

    w_state = expand(dt * jnp.exp(last - acs), 1).astype(BF16)
    w_off = expand(jnp.exp(acs), 1)
    decay = expand(jnp.broadcast_to(jnp.exp(last), (8, LANE)), 2)[0:1, :]

    xc = x_ref[0, pl.ds(r0, CHUNK), :]
    xw = xc * w_state
    x_zero = jnp.zeros((CHUNK, LANE), xc.dtype)
    for p in range(N_SSD_HEADS // 2):
        g = p // (N_SSD_HEADS // (2 * N_SSD_GROUPS))
        ps = slice(p * LANE, (p + 1) * LANE)
        xp = xc[:, ps]
        ms = []
        for half in range(2):
            lh = lane0 + 2 * p + half
            seg2 = acs2[:, lh:lh + 1] - acs2_t[lh:lh + 1, :]
            decay_mat = jnp.exp2(jnp.where(mask, seg2, MASKED))
            ms.append((cb[g] * decay_mat * dt_t[lh:lh + 1, :]).astype(BF16))
        x_heads = jnp.concatenate([jnp.where(low, xp, x_zero), jnp.where(low, x_zero, xp)], axis=0)
        y_diag = _dot(jnp.concatenate(ms, axis=1), x_heads)
        state = s_ref[p]
        y_off = _dot(c_grp[g], state.astype(BF16)) * w_off[:, ps]
        y_ref[0, pl.ds(r0, CHUNK), ps] = (y_diag + y_off).astype(y_ref.dtype)
        s_ref[p] = state * decay[:, ps] + _dot(b_t, xw[:, ps])


def _ssd_kernel(xf_ref, bcf_ref, dtf_ref, xb_ref, bcb_ref, dtb_ref, par_ref, ef_ref, eb_ref,
                yf_ref, yb_ref, sf_ref, sb_ref, *, cps):
    @pl.when(pl.program_id(1) == 0)
    def _():
        sf_ref[...] = jnp.zeros_like(sf_ref)
        sb_ref[...] = jnp.zeros_like(sb_ref)

    def body(j, carry):
        rf = pl.multiple_of(j * CHUNK, CHUNK)
        rb = pl.multiple_of((cps - 1 - j) * CHUNK, CHUNK)
        _ssd_chunk(xf_ref, bcf_ref, dtf_ref, par_ref, ef_ref, yf_ref, sf_ref, rf, fwd=True)
        _ssd_chunk(xb_ref, bcb_ref, dtb_ref, par_ref, eb_ref, yb_ref, sb_ref, rb, fwd=False)
        return carry

    lax.fori_loop(0, cps, body, 0, unroll=2)


def _ssd(xs, bc, dt, par, e_f, e_b, layer, cps):
    b, l, _ = xs.shape
    rows = cps * CHUNK
    ns = l // rows
    fmap = lambda bi, c: (bi, c, 0)
    bmap = lambda bi, c: (bi, ns - 1 - c, 0)
    const = lambda bi, c: (0, 0)
    state = pltpu.VMEM((N_SSD_HEADS // 2, LANE, LANE), F32)
    return pl.pallas_call(
        functools.partial(_ssd_kernel, cps=cps),
        grid=(b, ns),
        in_specs=[
            pl.BlockSpec((1, rows, D_INNER), fmap),
            pl.BlockSpec((1, rows, 2 * BC_DIM), fmap),
            pl.BlockSpec((1, rows, DT_COLS), fmap),
            pl.BlockSpec((1, rows, D_INNER), bmap),
            pl.BlockSpec((1, rows, 2 * BC_DIM), bmap),
            pl.BlockSpec((1, rows, DT_COLS), bmap),
            _layer_block((8, LANE), layer),
            pl.BlockSpec((LANE, D_INNER), const),
            pl.BlockSpec((LANE, D_INNER), const),
        ],
        out_specs=[pl.BlockSpec((1, rows, D_INNER), fmap), pl.BlockSpec((1, rows, D_INNER), bmap)],
        out_shape=[jax.ShapeDtypeStruct((b, l, D_INNER), BF16)] * 2,
        scratch_shapes=[state, state],
        compiler_params=_cparams("parallel", "arbitrary"),
        name="ssd_scan",
    )(xs, bc, dt, xs, bc, dt, par, e_f, e_b)


def _router_gates(lg_t):
    row = lambda j: lg_t[j:j + 1, :]
    lgrp = [row(j) for j in range(N_EGROUPS)]
    m = functools.reduce(jnp.maximum, lgrp)
    g_sel = jnp.where(lgrp[0] == m, 0, jnp.where(lgrp[1] == m, 1, jnp.where(lgrp[2] == m, 2, 3)))
    p_sel = 1.0 / sum(jnp.exp(v - m) for v in lgrp)
    s = []
    for k in range(EXPERTS_PER_GROUP):
        e = [row(N_EGROUPS + EXPERTS_PER_GROUP * g + k) for g in range(N_EGROUPS)]
        s.append(jnp.where(g_sel == 0, e[0], jnp.where(g_sel == 1, e[1], jnp.where(g_sel == 2, e[2], e[3]))))
    m1 = functools.reduce(jnp.maximum, s)
    i1 = jnp.where(s[0] == m1, 0, jnp.where(s[1] == m1, 1, jnp.where(s[2] == m1, 2, 3)))
    s2 = [jnp.where(i1 == k, -jnp.inf, s[k]) for k in range(EXPERTS_PER_GROUP)]
    m2 = functools.reduce(jnp.maximum, s2)
    i2 = jnp.where(s2[0] == m2, 0, jnp.where(s2[1] == m2, 1, jnp.where(s2[2] == m2, 2, 3)))
    t = jnp.exp(m2 - m1)
    w1 = 1.0 / (1.0 + t)
    w2 = t / (1.0 + t)
    first_is_a = i1 < i2
    a = jnp.minimum(i1, i2)
    b = jnp.maximum(i1, i2)
    gate_a = jnp.where(first_is_a, w1, w2) * p_sel
    gate_b = jnp.where(first_is_a, w2, w1) * p_sel
    pair = jnp.where(a == 0, 0, jnp.where(a == 1, 3, 5)) + b - a - 1
    bucket = g_sel * PAIRS_PER_GROUP + pair
    tm = lg_t.shape[1]
    rows = lax.broadcasted_iota(jnp.int32, (SUBLANES, tm), 0)
    gates = jnp.where(rows == 0, gate_a, jnp.where(rows == 1, gate_b, 0.0))
    return jnp.concatenate([gates, jnp.zeros((LANE - SUBLANES, tm), F32)], axis=0), bucket


def _pack_bf16_pairs(h):
    bits = lax.bitcast_convert_type(h.astype(BF16).astype(F32), jnp.uint32)
    half = h.shape[1] // 2
    return (bits[:, :half] >> 16) | (bits[:, half:] & jnp.uint32(0xFFFF0000))


def _unpack_bf16_pairs(w):
    lo = lax.bitcast_convert_type(w << 16, F32).astype(BF16)
    hi = lax.bitcast_convert_type(w & jnp.uint32(0xFFFF0000), F32).astype(BF16)
    return lo, hi


def _merge_kernel(yf_ref, yb_ref, xs_ref, z_ref, att_ref, gag_ref, x_ref, dsk_ref, ng_ref,
                  wso_ref, wao_ref, wo_ref, gf_ref, wrh_ref, wrhl_ref, rb_ref, upper_ref,
                  x1_ref, h2g_ref, grp_ref, rank_ref, counts_ref, cnt):
    @pl.when(pl.program_id(0) == 0)
    def _():
        cnt[...] = jnp.zeros_like(cnt)

    y = yf_ref[...].astype(F32) + yb_ref[...].astype(F32) + dsk_ref[...] * xs_ref[...].astype(F32)
    z = z_ref[...].astype(F32)
    y = y * (z * _sigmoid(z))
    gsz = D_INNER // N_SSD_GROUPS
    parts = []
    for g in range(N_SSD_GROUPS):
        yg = y[:, g * gsz:(g + 1) * gsz]
        parts.append(yg * lax.rsqrt(jnp.mean(yg * yg, axis=-1, keepdims=True) + EPS))
    ysn = (jnp.concatenate(parts, axis=1) * ng_ref[...]).astype(BF16)
    ys = _dot(ysn, wso_ref[...])
    ya = _dot(att_ref[...], wao_ref[...])
    ga = gag_ref[:, 0:D_MODEL].astype(F32)
    gs = gag_ref[:, D_MODEL:2 * D_MODEL].astype(F32)
    mixed = _sigmoid(ga) * ya + _sigmoid(gs) * ys
    x1 = x_ref[...] + _dot(mixed.astype(BF16), wo_ref[...])
    x1_ref[...] = x1
    h2 = _rmsnorm(x1, gf_ref[...])
    h2g_ref[:, 0:ROUTED_WORDS] = _pack_bf16_pairs(h2)
    hh, hl = _split_bf16(h2, 2)
    hi_terms = _dot(hh, wrhl_ref[...])
    lg = hi_terms[:, 0:LANE] + _dot(hl, wrh_ref[...]) + hi_terms[:, LANE:] + rb_ref[...]
    tm = x1.shape[0]
    gates_t, bucket = _router_gates(lg.T)
    h2g_ref[:, ROUTED_WORDS:] = lax.bitcast_convert_type(gates_t.T, jnp.uint32)
    onehot = (lax.broadcasted_iota(jnp.int32, (BUCKET_ROWS, tm), 0) == bucket).astype(F32)
    rank_all = _dot(onehot.astype(BF16), upper_ref[...]) + cnt[:, 0:1]
    rank_ref[0] = jnp.sum(onehot * rank_all, axis=0, keepdims=True).astype(jnp.int32)
    grp_ref[0] = bucket
    cnt[...] += jnp.sum(onehot, axis=1, keepdims=True)
    counts_ref[...] = cnt[...]


def _merge(yf, yb, xs, z, att, gag, x, dsk, ng, wso, wao, wo, gf, wrh, wrhl, rb, layer, tm):
    t = x.shape[0]
    nt = t // tm
    upper = jnp.asarray(np.triu(np.ones((tm, tm), np.float32), 1), BF16)
    tile = lambda n: pl.BlockSpec((tm, n), lambda i: (i, 0))
    per_layer = lambda a: _layer_block(a.shape[1:], layer)
    row = pl.BlockSpec((1, 1, tm), lambda i: (i, 0, 0))
    consts = (dsk, ng, wso, wao, wo, gf, wrh, wrhl, rb)
    return pl.pallas_call(
        _merge_kernel,
        grid=(nt,),
        in_specs=[tile(D_INNER), tile(D_INNER), tile(D_INNER), tile(D_INNER), tile(Q_DIM),
                  tile(GATE_COLS), tile(D_MODEL)] + [per_layer(a) for a in consts]
                 + [pl.BlockSpec((tm, tm), lambda i: (0, 0))],
        out_specs=[tile(D_MODEL), tile(ROUTED_WORDS + LANE), row, row, pl.BlockSpec((BUCKET_ROWS, LANE), lambda i: (0, 0))],
        out_shape=[jax.ShapeDtypeStruct((t, D_MODEL), F32),
                   jax.ShapeDtypeStruct((t, ROUTED_WORDS + LANE), jnp.uint32),
                   jax.ShapeDtypeStruct((nt, 1, tm), jnp.int32),
                   jax.ShapeDtypeStruct((nt, 1, tm), jnp.int32),
                   jax.ShapeDtypeStruct((BUCKET_ROWS, LANE), F32)],
        scratch_shapes=[pltpu.VMEM((BUCKET_ROWS, LANE), F32)],
        compiler_params=_cparams("arbitrary"),
        name="merge_router",
    )(yf, yb, xs, z, att, gag, x, *consts, upper)


def _dma_params(*sem):
    return pltpu.CompilerParams(dimension_semantics=sem, vmem_limit_bytes=VMEM_LIMIT_BYTES,
                                disable_bounds_checks=True)


def _permute_kernel(pos_ref, src_ref, init_ref, dst_ref, stage, sem, *, rows, nt):
    del init_ref
    i = pl.program_id(0)
    slot = i % 2
    stage[slot] = src_ref[...]

    def issue(r8, carry):
        base = pl.multiple_of(r8 * SUBLANES, SUBLANES)
        src_tile = stage.at[slot, pl.ds(base, SUBLANES)]
        for k in range(SUBLANES):
            pltpu.make_async_copy(src_tile.at[pl.ds(k, 1)], dst_ref.at[pl.ds(pos_ref[0, 0, base + k], 1)],
                                  sem.at[slot]).start()
        return carry

    lax.fori_loop(0, rows // SUBLANES, issue, 0)

    def wait_all(s):
        pltpu.make_async_copy(stage.at[s], dst_ref.at[pl.ds(0, rows)], sem.at[s]).wait()

    @pl.when(i > 0)
    def _():
        wait_all(1 - slot)

    @pl.when(i == nt - 1)
    def _():
        wait_all(slot)


def _permute(h2g, pos, t_pad, rows):
    t, w = h2g.shape
    nt = t // rows
    return pl.pallas_call(
        functools.partial(_permute_kernel, rows=rows, nt=nt),
        grid=(nt,),
        in_specs=[pl.BlockSpec((1, 1, rows), lambda i: (i, 0, 0), memory_space=pltpu.SMEM),
                  pl.BlockSpec((rows, w), lambda i: (i, 0)),
                  pl.BlockSpec(memory_space=pl.ANY)],
        out_specs=pl.BlockSpec(memory_space=pl.ANY),
        out_shape=jax.ShapeDtypeStruct((t_pad, w), h2g.dtype),
        scratch_shapes=[pltpu.VMEM((2, rows, w), h2g.dtype), pltpu.SemaphoreType.DMA((2,))],
        input_output_aliases={2: 0},
        compiler_params=_dma_params("arbitrary"),
        name="moe_permute",
    )(pos.reshape(nt, 1, rows), h2g, jnp.zeros((t_pad, w), h2g.dtype))


def _experts_kernel(ea_ref, eb_ref, tv_ref, hs_ref, wgu_a, wd_a, wgu_b, wd_b, o_ref):
    del ea_ref, eb_ref
    i = pl.program_id(0)

    @pl.when(tv_ref[i] == 0)
    def _():
        o_ref[...] = jnp.zeros_like(o_ref)

    @pl.when(tv_ref[i] != 0)
    def _():
        lo, hi = _unpack_bf16_pairs(hs_ref[:, 0:ROUTED_WORDS])
        gates = lax.bitcast_convert_type(hs_ref[:, ROUTED_WORDS:], F32)
        acc = jnp.zeros(o_ref.shape, F32)
        for lane, (wgu, wd) in enumerate(((wgu_a, wd_a), (wgu_b, wd_b))):
            gu = _dot(lo, wgu[0, 0:ROUTED_WORDS, :]) + _dot(hi, wgu[0, ROUTED_WORDS:, :])
            g = gu[:, 0:D_EXPERT]
            u = gu[:, D_EXPERT:2 * D_EXPERT]
            a = (g * _sigmoid(g)) * u * gates[:, lane:lane + 1]
            acc = acc + _dot(a.astype(BF16), wd[0])
        o_ref[...] = acc


def _experts(hs, expert_a, expert_b, tile_valid, wgu, wd, layer, tms):
    t_pad, w = hs.shape
    pick_a = lambda i, ea, eb, tv: (layer, ea[i], 0, 0)
    pick_b = lambda i, ea, eb, tv: (layer, eb[i], 0, 0)
    return pl.pallas_call(
        _experts_kernel,
        grid_spec=pltpu.PrefetchScalarGridSpec(
            num_scalar_prefetch=3,
            grid=(t_pad // tms,),
            in_specs=[
                pl.BlockSpec((tms, w), lambda i, ea, eb, tv: (i, 0)),
                pl.BlockSpec((None, 1, D_MODEL, 2 * D_EXPERT), pick_a),
                pl.BlockSpec((None, 1, D_EXPERT, D_MODEL), pick_a),
                pl.BlockSpec((None, 1, D_MODEL, 2 * D_EXPERT), pick_b),
                pl.BlockSpec((None, 1, D_EXPERT, D_MODEL), pick_b),
            ],
            out_specs=pl.BlockSpec((tms, D_MODEL), lambda i, ea, eb, tv: (i, 0)),
        ),
        out_shape=jax.ShapeDtypeStruct((t_pad, D_MODEL), F32),
        compiler_params=_cparams("arbitrary"),
        name="moe_experts",
    )(expert_a, expert_b, tile_valid, hs, wgu, wd, wgu, wd)


def _ple_kernel(pos_ref, posn_ref, x_ref, p_ref, g_ref, wg_ref, wp_ref, gfin_ref, ys_ref, o_ref, buf, sem,
                *, tm, nt, final):
    i = pl.program_id(0)
    slot = i % 2

    def gather(pref, s):
        def issue(r8, carry):
            base = pl.multiple_of(r8 * SUBLANES, SUBLANES)
            dst_tile = buf.at[s, pl.ds(base, SUBLANES)]
            for k in range(SUBLANES):
                pltpu.make_async_copy(ys_ref.at[pl.ds(pref[0, 0, base + k], 1)], dst_tile.at[pl.ds(k, 1)],
                                      sem.at[s]).start()
            return carry
        lax.fori_loop(0, tm // SUBLANES, issue, 0)

    @pl.when(i == 0)
    def _():
        gather(pos_ref, 0)

    @pl.when(i + 1 < nt)
    def _():
        gather(posn_ref, 1 - slot)

    pltpu.make_async_copy(ys_ref.at[pl.ds(0, tm)], buf.at[slot], sem.at[slot]).wait()
    x = x_ref[...] + buf[slot]
    gate = _sigmoid(_dot(_rmsnorm(x, g_ref[...]).astype(BF16), wg_ref[...]))
    y = x + gate * _dot(p_ref[...].astype(BF16), wp_ref[...])
    o_ref[...] = _rmsnorm(y, gfin_ref[...]) if final else y


def _ple(x, ys, pos, p, g, wg, wp, g_final, layer, final, tm):
    t = x.shape[0]
    nt = t // tm
    pos3 = pos.reshape(nt, 1, tm)
    return pl.pallas_call(
        functools.partial(_ple_kernel, tm=tm, nt=nt, final=final),
        grid=(nt,),
        in_specs=[
            pl.BlockSpec((1, 1, tm), lambda i: (i, 0, 0), memory_space=pltpu.SMEM),
            pl.BlockSpec((1, 1, tm), lambda i: (jnp.minimum(i + 1, nt - 1), 0, 0), memory_space=pltpu.SMEM),
            pl.BlockSpec((tm, D_MODEL), lambda i: (i, 0)),
            pl.BlockSpec((None, tm, PLE_DIM), lambda i: (layer, i, 0)),
            _layer_block((1, D_MODEL), layer),
            _layer_block((D_MODEL, D_MODEL), layer),
            _layer_block((PLE_DIM, D_MODEL), layer),
            pl.BlockSpec((1, D_MODEL), lambda i: (0, 0)),
            pl.BlockSpec(memory_space=pl.ANY),
        ],
        out_specs=pl.BlockSpec((tm, D_MODEL), lambda i: (i, 0)),
        out_shape=jax.ShapeDtypeStruct((t, D_MODEL), F32),
        scratch_shapes=[pltpu.VMEM((2, tm, D_MODEL), F32), pltpu.SemaphoreType.DMA((2,))],
        compiler_params=_dma_params("arbitrary"),
        name="ple_gate",
    )(pos3, pos3, x, p, g, wg, wp, g_final, ys)


def _route(bucket, rank, counts, t, tms):
    cnt = counts[:N_BUCKETS_MOE, 0].astype(jnp.int32)
    padded = (cnt + tms - 1) // tms * tms
    ends = jnp.cumsum(padded)
    starts = ends - padded
    bucket = bucket.reshape(t)
    pos = rank.reshape(t) + sum(jnp.where(bucket == k, starts[k], 0) for k in range(N_BUCKETS_MOE))
    t_pad = t + N_BUCKETS_MOE * tms
    tile_start = jnp.arange(t_pad // tms, dtype=jnp.int32) * tms
    tile_bucket = jnp.minimum(sum((tile_start >= ends[k]).astype(jnp.int32) for k in range(N_BUCKETS_MOE)),
                              N_BUCKETS_MOE - 1)
    tile_valid = (tile_start < ends[N_BUCKETS_MOE - 1]).astype(jnp.int32)
    pairs = [(a, b) for a in range(EXPERTS_PER_GROUP) for b in range(a + 1, EXPERTS_PER_GROUP)]
    group, pair = tile_bucket // PAIRS_PER_GROUP, tile_bucket % PAIRS_PER_GROUP
    first = sum(jnp.where(pair == k, a, 0) for k, (a, _) in enumerate(pairs))
    second = sum(jnp.where(pair == k, b, 0) for k, (_, b) in enumerate(pairs))
    expert_a = (group * EXPERTS_PER_GROUP + first).astype(jnp.int32)
    expert_b = (group * EXPERTS_PER_GROUP + second).astype(jnp.int32)
    return pos, expert_a, expert_b, tile_valid, t_pad


def _band_bias(rel_bias):
    rel = (np.arange(3 * BLK)[None, :] - BLK) - np.arange(BLK)[:, None]
    nb = N_BUCKETS // 2
    max_exact = nb // 2
    ret = (rel > 0).astype(np.int32) * nb
    n = np.abs(rel)
    large = max_exact + (np.log(np.maximum(n, 1) / max_exact) / np.log(MAX_DISTANCE / max_exact)
                         * (nb - max_exact)).astype(np.int32)
    large = np.minimum(large, nb - 1)
    bucket = (ret + np.where(n < max_exact, n, large)).astype(np.int32)
    table = rel_bias.astype(F32)
    bucket = jnp.asarray(bucket)[None]
    bias = jnp.zeros((N_HEADS, BLK, 3 * BLK), F32)
    for k in range(N_BUCKETS):
        bias = jnp.where(bucket == k, table[k][:, None, None], bias)
    bias = jnp.where(jnp.asarray(n <= WINDOW)[None], bias, MASKED)
    bias = bias[np.asarray(STACK_HEAD_ORDER)].reshape(N_HEADS * BLK, 3 * BLK)
    key_blk = jnp.arange(3 * BLK) // BLK
    no_left = jnp.where(key_blk == 0, MASKED, bias)
    no_right = jnp.where(key_blk == 2, MASKED, bias)
    return jnp.stack([bias, no_left, no_right, jnp.where(key_blk == 2, MASKED, no_left)])


def _by_head(a, axis):
    return jnp.concatenate([lax.slice_in_dim(a, h * HEAD_DIM, (h + 1) * HEAD_DIM, axis=axis)
                            for h in STACK_HEAD_ORDER], axis=axis)


def _head_expand_matrix(first_lane):
    e = np.zeros((LANE, D_INNER), np.float32)
    for h in range(N_SSD_HEADS):
        e[first_lane + h, h * SSD_HEAD_DIM:(h + 1) * SSD_HEAD_DIM] = 1.0
    return jnp.asarray(e, BF16)


def _prep_layers(P):
    depth = P["w_in"].shape[0]
    w = P["w_in"]
    o = np.cumsum([0, Q_DIM, KV_DIM, KV_DIM, D_INNER, CONV_DIM, N_SSD_HEADS, N_SSD_HEADS, D_MODEL, D_MODEL])
    q, k, v, z, xbc, dtf, dtb, ga, gs = [w[:, :, o[i]:o[i + 1]] for i in range(9)]
    dt_pad = jnp.zeros((depth, D_MODEL, DT_COLS - 2 * N_SSD_HEADS), F32)
    w_in = jnp.concatenate([_by_head(q, 2) * (HEAD_DIM ** -0.5), k, v, z, xbc, ga, gs, dtf, dtb, dt_pad],
                           axis=-1).astype(BF16)

    pad_lanes = lambda a: jnp.pad(a, ((0, 0), (0, LANE - a.shape[-1])))
    par = jnp.stack([pad_lanes(jnp.concatenate([P["dt_bias_f"], P["dt_bias_b"]], axis=-1)),
                     pad_lanes(jnp.concatenate([P["a_log_f"], P["a_log_b"]], axis=-1))], axis=1)
    par = jnp.pad(par.astype(F32), ((0, 0), (0, 6), (0, 0)))

    wr = jnp.concatenate([P["router_g"], P["router_e"]], axis=-1)
    wr = jnp.pad(wr, ((0, 0), (0, 0), (0, LANE - wr.shape[-1])))
    wr_hi = wr.astype(BF16)
    wr_hilo = jnp.concatenate([wr_hi, (wr - wr_hi.astype(F32)).astype(BF16)], axis=-1)
    rb = pad_lanes(jnp.concatenate([P["router_g_b"], P["router_e_b"]], axis=-1))[:, None, :]

    row = lambda a: a.astype(F32)[:, None, :]
    return dict(
        g_mix=row(P["g_mix"]), w_in=w_in, sink=jnp.broadcast_to(jnp.repeat(P["attn_sink"].astype(F32)[:, np.asarray(STACK_HEAD_ORDER)], BLK, axis=1)[:, :, None],
                              (depth, N_HEADS * BLK, LANE)),
        w_attn_out=_by_head(P["w_attn_out"], 1).astype(BF16),
        conv_w=P["conv_w"].astype(F32), conv_b=row(P["conv_b"]), par=par,
        d_skip=row(jnp.repeat(P["d_skip"], SSD_HEAD_DIM, axis=-1)), ssd_norm_g=row(P["ssd_norm_g"]),
        w_ssd_out=P["w_ssd_out"].astype(BF16), w_out=P["w_out"].astype(BF16),
        g_ffn=row(P["g_ffn"]), wr_hi=wr_hi, wr_hilo=wr_hilo, rb=rb.astype(F32),
        w_gate_up=P["w_gate_up"].astype(BF16), w_down=P["w_down"].astype(BF16),
        g_ple=row(P["g_ple"]), w_ple_gate=P["w_ple_gate"].astype(BF16),
        w_ple_proj=P["w_ple_proj"].astype(BF16),
    )


def _tile(n, want):
    t = min(n, want)
    assert n % t == 0
    return t


def _trunk(x, p, layers, bias, e_f, e_b, g_final):
    b, l, _ = x.shape
    t = b * l
    tm = _tile(t, 512)
    seq_tile = _tile(l, 1024)
    cps = seq_tile // CHUNK

    W = layers
    depth = p.shape[0]
    p = p.reshape(depth, t, PLE_DIM)
    g_final = g_final.astype(F32)[None, :]
    xf = x.reshape(t, D_MODEL)
    for i in range(depth):
        qkv, z, gag, dt, xs, bc = _inproj(xf, W["g_mix"], W["w_in"], W["conv_w"], W["conv_b"], i, tm, l)
        att = _attention(qkv.reshape(b, l, QKV_COLS), W["sink"], bias, i, seq_tile)
        yf, yb = _ssd(xs.reshape(b, l, D_INNER), bc.reshape(b, l, 2 * BC_DIM), dt.reshape(b, l, DT_COLS),
                      W["par"], e_f, e_b, i, cps)
        x1, h2g, grp, rank, counts = _merge(
            yf.reshape(t, D_INNER), yb.reshape(t, D_INNER), xs.reshape(t, D_INNER), z, att.reshape(t, Q_DIM), gag,
            xf, W["d_skip"], W["ssd_norm_g"], W["w_ssd_out"], W["w_attn_out"], W["w_out"], W["g_ffn"],
            W["wr_hi"], W["wr_hilo"], W["rb"], i, tm)
        pos, expert_a, expert_b, tile_valid, t_pad = _route(grp, rank, counts, t, tm)
        hs = _permute(h2g, pos, t_pad, tm)
        ys = _experts(hs, expert_a, expert_b, tile_valid, W["w_gate_up"], W["w_down"], i, tm)
        xf = _ple(x1, ys, pos, p, W["g_ple"], W["w_ple_gate"], W["w_ple_proj"], g_final, i, i == depth - 1,
                  _tile(t, 1024))
    return xf.reshape(b, l, D_MODEL)


def kernel(x_prompt, x_sample, p_prompt, p_sample, g_mix, w_in, rel_bias, attn_sink, w_attn_out, conv_w, conv_b, a_log_f, a_log_b, dt_bias_f, dt_bias_b, d_skip, ssd_norm_g, w_ssd_out, w_out, g_ffn, router_g, router_g_b, router_e, router_e_b, w_gate_up, w_down, g_ple, w_ple_gate, w_ple_proj, g_final):
    P = dict(g_mix=g_mix, w_in=w_in, attn_sink=attn_sink, w_attn_out=w_attn_out, conv_w=conv_w, conv_b=conv_b,
             a_log_f=a_log_f, a_log_b=a_log_b, dt_bias_f=dt_bias_f, dt_bias_b=dt_bias_b, d_skip=d_skip,
             ssd_norm_g=ssd_norm_g, w_ssd_out=w_ssd_out, w_out=w_out, g_ffn=g_ffn, router_g=router_g,
             router_g_b=router_g_b, router_e=router_e, router_e_b=router_e_b, w_gate_up=w_gate_up,
             w_down=w_down, g_ple=g_ple, w_ple_gate=w_ple_gate, w_ple_proj=w_ple_proj)
    layers = _prep_layers(P)
    bias = _band_bias(rel_bias)
    e_f = _head_expand_matrix(0)
    e_b = _head_expand_matrix(N_SSD_HEADS)
    y_prompt = _trunk(x_prompt, p_prompt, layers, bias, e_f, e_b, g_final)
    y_sample = _trunk(x_sample, p_sample, layers, bias, e_f, e_b, g_final)
    return (y_prompt, y_sample)
```

```python
import functools

import numpy as np
import jax
import jax.numpy as jnp
from jax import lax
from jax.experimental import pallas as pl
from jax.experimental.pallas import tpu as pltpu

F32 = jnp.float32
BF16 = jnp.bfloat16

LANE = 128
SUBLANES = 8
HALF_LANE = LANE // 2
VMEM_LIMIT_BYTES = 56 * 1024 * 1024

D_MODEL = 1024
HEAD_DIM = 64
N_HEADS = 8
N_KV_HEADS = 2
GQ = N_HEADS // N_KV_HEADS
Q_DIM = N_HEADS * HEAD_DIM
KV_DIM = N_KV_HEADS * HEAD_DIM
WINDOW = 128
BLK = 128
N_BUCKETS = 32
MAX_DISTANCE = 128
D_INNER = 1024
SSD_HEAD_DIM = 64
N_SSD_HEADS = D_INNER // SSD_HEAD_DIM
N_SSD_GROUPS = 2
D_STATE = 64
CHUNK = 128
CONV_WIDTH = 5
CONV_PAD = CONV_WIDTH // 2
BC_DIM = N_SSD_GROUPS * D_STATE
CONV_DIM = D_INNER + 2 * BC_DIM
N_EGROUPS = 4
EXPERTS_PER_GROUP = 4
N_EXPERTS = N_EGROUPS * EXPERTS_PER_GROUP
PAIRS_PER_GROUP = EXPERTS_PER_GROUP * (EXPERTS_PER_GROUP - 1) // 2
N_BUCKETS_MOE = N_EGROUPS * PAIRS_PER_GROUP
BUCKET_ROWS = 32
D_EXPERT = 256
PLE_DIM = 256
EPS = 1e-6
MASKED = -1e30
LOG2_E = 1.4426950408889634

QKV_COLS = Q_DIM + 2 * KV_DIM
GATE_COLS = 2 * D_MODEL
DT_COLS = LANE
IN_COLS_PADDED = QKV_COLS + D_INNER + CONV_DIM + GATE_COLS + DT_COLS
XBC_OFFSET = QKV_COLS + D_INNER
HALO = 16
ROUTED_WORDS = D_MODEL // 2


def _cparams(*sem):
    return pltpu.CompilerParams(dimension_semantics=sem, vmem_limit_bytes=VMEM_LIMIT_BYTES)


def _layer_block(shape, layer):
    return pl.BlockSpec((None,) + tuple(shape), lambda *_: (layer,) + (0,) * len(shape))


def _dot(a, b):
    return jnp.dot(a, b, preferred_element_type=F32)


def _dot_nt(a, b):
    return lax.dot_general(a, b, (((1,), (1,)), ((), ())), preferred_element_type=F32)


def _split_bf16(v, terms):
    parts = []
    r = v
    for _ in range(terms):
        p = r.astype(BF16)
        parts.append(p)
        r = r - p.astype(F32)
    return parts


def _rmsnorm(x, g):
    return x * lax.rsqrt(jnp.mean(x * x, axis=-1, keepdims=True) + EPS) * g


def _sigmoid(x):
    return 1.0 / (1.0 + jnp.exp(-x))


def _inproj_kernel(xp_ref, x_ref, xn_ref, g_ref, w_ref, cw_ref, cb_ref,
                   qkv_ref, z_ref, gag_ref, dt_ref, xs_ref, bc_ref, ext, *, tm, tiles_per_seq):
    g = g_ref[...]
    hb = _rmsnorm(x_ref[...], g).astype(BF16)
    h_ext = jnp.concatenate([_rmsnorm(xp_ref[...], g).astype(BF16), hb,
                             _rmsnorm(xn_ref[...], g).astype(BF16)], axis=0)
    xbc = _dot(h_ext, w_ref[:, XBC_OFFSET:XBC_OFFSET + CONV_DIM])
    pos_in_seq = pl.program_id(0) % tiles_per_seq
    first = pos_in_seq == 0
    last = pos_in_seq == tiles_per_seq - 1
    n_planes = CONV_DIM // LANE
    for cb in range(n_planes):
        cs = slice(cb * LANE, (cb + 1) * LANE)
        ext[cb, 0:HALO] = jnp.where(first, 0.0, xbc[0:HALO, cs])
        ext[cb, HALO:HALO + tm] = xbc[HALO:HALO + tm, cs]
        ext[cb, HALO + tm:] = jnp.where(last, 0.0, xbc[HALO + tm:, cs])

    def conv_plane(cb):
        cs = slice(cb * LANE, (cb + 1) * LANE)
        acc = cb_ref[:, cs] + cw_ref[0:1, cs] * ext[cb, HALO - CONV_PAD:HALO - CONV_PAD + tm]
        for k in range(1, CONV_WIDTH):
            acc = acc + cw_ref[k:k + 1, cs] * ext[cb, HALO - CONV_PAD + k:HALO - CONV_PAD + k + tm]
        y = (acc * _sigmoid(acc))
        if cb < D_INNER // LANE:
            xs_ref[:, cs] = y.astype(xs_ref.dtype)
        else:
            c0 = cb * LANE - D_INNER
            bc_ref[:, c0:c0 + LANE] = y.astype(bc_ref.dtype)

    segments = ((qkv_ref, 0), (z_ref, QKV_COLS), (gag_ref, XBC_OFFSET + CONV_DIM),
                (dt_ref, XBC_OFFSET + CONV_DIM + GATE_COLS))
    planes_after = (3, 3, 4, 0)
    cb = 0
    for (ref, off), n_conv in zip(segments, planes_after):
        ref[...] = _dot(hb, w_ref[:, off:off + ref.shape[-1]]).astype(ref.dtype)
        for _ in range(n_conv):
            conv_plane(cb)
            cb += 1
    assert cb == n_planes


def _inproj(x, g, w, conv_w, conv_b, layer, tm, seq):
    t = x.shape[0]
    r = tm // HALO
    nh = t // HALO
    widths = (QKV_COLS, D_INNER, GATE_COLS, DT_COLS, D_INNER, 2 * BC_DIM)
    dtypes = (BF16, BF16, BF16, F32, BF16, BF16)
    return pl.pallas_call(
        functools.partial(_inproj_kernel, tm=tm, tiles_per_seq=seq // tm),
        grid=(t // tm,),
        in_specs=[
            pl.BlockSpec((HALO, D_MODEL), lambda i: (jnp.maximum(i * r - 1, 0), 0)),
            pl.BlockSpec((tm, D_MODEL), lambda i: (i, 0)),
            pl.BlockSpec((HALO, D_MODEL), lambda i: (jnp.minimum((i + 1) * r, nh - 1), 0)),
            _layer_block((1, D_MODEL), layer),
            _layer_block((D_MODEL, IN_COLS_PADDED), layer),
            _layer_block((CONV_WIDTH, CONV_DIM), layer),
            _layer_block((1, CONV_DIM), layer),
        ],
        out_specs=[pl.BlockSpec((tm, n), lambda i: (i, 0)) for n in widths],
        out_shape=[jax.ShapeDtypeStruct((t, n), d) for n, d in zip(widths, dtypes)],
        scratch_shapes=[pltpu.VMEM((CONV_DIM // LANE, tm + 2 * HALO, LANE), F32)],
        compiler_params=_cparams("parallel"),
        name="inproj_conv",
    )(x, x, x, g, w, conv_w, conv_b)


STACK_HEAD_ORDER = tuple(half * GQ + p for p in range(GQ) for half in range(N_KV_HEADS))


SUBS_PER_TRIP = 4


def _attn_kernel(q_ref, kp_ref, km_ref, kn_ref, vp_ref, vm_ref, vn_ref, bias_ref, sink_ref, o_ref,
                 k_ext, v_ext, *, qb, nq):
    n = pl.program_id(1)
    k_ext[0:BLK] = kp_ref[0]
    k_ext[BLK:BLK + qb] = km_ref[0]
    k_ext[BLK + qb:] = kn_ref[0]
    v_ext[0:BLK, 0:KV_DIM] = vp_ref[0]
    v_ext[BLK:BLK + qb, 0:KV_DIM] = vm_ref[0]
    v_ext[BLK + qb:, 0:KV_DIM] = vn_ref[0]
    v_ext[:, KV_DIM:] = jnp.ones((qb + 2 * BLK, LANE), v_ext.dtype)
    low = lax.broadcasted_iota(jnp.int32, (BLK, LANE), 1) < HALF_LANE
    last_sub = qb // BLK - 1

    def sub_block(j):
        r0 = pl.multiple_of(j * BLK, BLK)
        q = q_ref[0, pl.ds(r0, BLK), :]
        zero = jnp.zeros((BLK, LANE), q.dtype)
        parts = []
        for p in range(GQ):
            qp = q[:, p * LANE:(p + 1) * LANE]
            parts += [jnp.where(low, qp, zero), jnp.where(low, zero, qp)]
        qs = jnp.concatenate(parts, axis=0)
        first = jnp.logical_and(n == 0, j == 0).astype(jnp.int32)
        last = jnp.logical_and(n == nq - 1, j == last_sub).astype(jnp.int32)
        s = _dot_nt(qs, k_ext[pl.ds(r0, 3 * BLK), :]) + bias_ref[first + 2 * last]
        sc = [s[:, c * BLK:(c + 1) * BLK] for c in range(3)]
        sk = sink_ref[...]
        row_max = jnp.max(jnp.maximum(jnp.maximum(sc[0], sc[1]), sc[2]), axis=-1, keepdims=True)
        m = jnp.maximum(row_max, sk)
        p_un = jnp.concatenate([jnp.exp(c - m) for c in sc], axis=1).astype(BF16)
        ov = _dot(p_un, v_ext[pl.ds(r0, 3 * BLK), :])
        o = ov[:, 0:KV_DIM] / (ov[:, KV_DIM:] + jnp.exp(sk - m))
        for p in range(GQ):
            o_lo = o[(2 * p) * BLK:(2 * p + 1) * BLK]
            o_hi = o[(2 * p + 1) * BLK:(2 * p + 2) * BLK]
            o_ref[0, pl.ds(r0, BLK), p * LANE:(p + 1) * LANE] = jnp.where(low, o_lo, o_hi).astype(o_ref.dtype)

    def trip(jj, carry):
        for u in range(SUBS_PER_TRIP):
            sub_block(jj * SUBS_PER_TRIP + u)
        return carry

    lax.fori_loop(0, qb // (BLK * SUBS_PER_TRIP), trip, 0)


def _attention(qkv, sink_rows, bias, layer, qb):
    b, l, _ = qkv.shape
    nq = l // qb
    r = qb // BLK
    nblk = l // BLK
    kcol, vcol = Q_DIM // LANE, Q_DIM // LANE + 1

    def prev_map(c):
        return lambda bi, n: (bi, jnp.maximum(n * r - 1, 0), c)

    def next_map(c):
        return lambda bi, n: (bi, jnp.minimum((n + 1) * r, nblk - 1), c)

    return pl.pallas_call(
        functools.partial(_attn_kernel, qb=qb, nq=nq),
        grid=(b, nq),
        in_specs=[
            pl.BlockSpec((1, qb, Q_DIM), lambda bi, n: (bi, n, 0)),
            pl.BlockSpec((1, BLK, KV_DIM), prev_map(kcol)),
            pl.BlockSpec((1, qb, KV_DIM), lambda bi, n: (bi, n, kcol)),
            pl.BlockSpec((1, BLK, KV_DIM), next_map(kcol)),
            pl.BlockSpec((1, BLK, KV_DIM), prev_map(vcol)),
            pl.BlockSpec((1, qb, KV_DIM), lambda bi, n: (bi, n, vcol)),
            pl.BlockSpec((1, BLK, KV_DIM), next_map(vcol)),
            pl.BlockSpec((4, N_HEADS * BLK, 3 * BLK), lambda bi, n: (0, 0, 0)),
            _layer_block((N_HEADS * BLK, LANE), layer),
        ],
        out_specs=pl.BlockSpec((1, qb, Q_DIM), lambda bi, n: (bi, n, 0)),
        out_shape=jax.ShapeDtypeStruct((b, l, Q_DIM), BF16),
        scratch_shapes=[pltpu.VMEM((qb + 2 * BLK, KV_DIM), BF16),
                        pltpu.VMEM((qb + 2 * BLK, KV_DIM + LANE), BF16)],
        compiler_params=_cparams("parallel", "parallel"),
        name="attention",
    )(qkv, qkv, qkv, qkv, qkv, qkv, qkv, bias, sink_rows)


def _ssd_chunk(x_ref, bc_ref, dt_ref, par_ref, e_ref, y_ref, s_ref, r0, *, fwd):
    rows = lax.broadcasted_iota(jnp.int32, (CHUNK, CHUNK), 0)
    cols = lax.broadcasted_iota(jnp.int32, (CHUNK, CHUNK), 1)
    mask = rows >= cols if fwd else rows <= cols
    low = cols < HALF_LANE
    lane0 = 0 if fwd else N_SSD_HEADS

    x = dt_ref[0, pl.ds(r0, CHUNK), :] + par_ref[0:1, :]
    dt = jnp.maximum(x, 0.0) + jnp.log(1.0 + jnp.exp(-jnp.abs(x)))
    a = dt * (-jnp.exp(par_ref[1:2, :]))
    tri = mask.astype(BF16)
    acs = sum(_dot(tri, part) for part in _split_bf16(a, 3))
    last = acs[CHUNK - 1:CHUNK, :] if fwd else acs[0:1, :]
    acs2 = acs * LOG2_E
    acs2_t = acs2.T
    dt_t = dt.T

    bc = bc_ref[0, pl.ds(r0, CHUNK), :]
    b_all = bc[:, 0:BC_DIM]
    c_all = bc[:, BC_DIM:2 * BC_DIM]
    b_t = b_all.astype(F32).T.astype(BF16)
    zero = jnp.zeros_like(c_all)
    c_grp = [jnp.where(low, c_all, zero), jnp.where(low, zero, c_all)]
    cb = [_dot_nt(c, b_all) for c in c_grp]

    def expand(v, terms):
        return sum(_dot(part, e_ref[...]) for part in _split_bf16(v, terms))

    w_state = expand(dt * jnp.exp(last - acs), 1).astype(BF16)
    w_off = expand(jnp.exp(acs), 1)
    decay = expand(jnp.broadcast_to(jnp.exp(last), (8, LANE)), 2)[0:1, :]

    xc = x_ref[0, pl.ds(r0, CHUNK), :]
    xw = xc * w_state
    x_zero = jnp.zeros((CHUNK, LANE), xc.dtype)
    for p in range(N_SSD_HEADS // 2):
        g = p // (N_SSD_HEADS // (2 * N_SSD_GROUPS))
        ps = slice(p * LANE, (p + 1) * LANE)
        xp = xc[:, ps]
        ms = []
        for half in range(2):
            lh = lane0 + 2 * p + half
            seg2 = acs2[:, lh:lh + 1] - acs2_t[lh:lh + 1, :]
            decay_mat = jnp.exp2(jnp.where(mask, seg2, MASKED))
            ms.append((cb[g] * decay_mat * dt_t[lh:lh + 1, :]).astype(BF16))
        x_heads = jnp.concatenate([jnp.where(low, xp, x_zero), jnp.where(low, x_zero, xp)], axis=0)
        y_diag = _dot(jnp.concatenate(ms, axis=1), x_heads)
        state = s_ref[p]
        y_off = _dot(c_grp[g], state.astype(BF16)) * w_off[:, ps]
        y_ref[0, pl.ds(r0, CHUNK), ps] = (y_diag + y_off).astype(y_ref.dtype)
        s_ref[p] = state * decay[:, ps] + _dot(b_t, xw[:, ps])


def _ssd_kernel(xf_ref, bcf_ref, dtf_ref, xb_ref, bcb_ref, dtb_ref, par_ref, ef_ref, eb_ref,
                yf_ref, yb_ref, sf_ref, sb_ref, *, cps):
    @pl.when(pl.program_id(1) == 0)
    def _():
        sf_ref[...] = jnp.zeros_like(sf_ref)
        sb_ref[...] = jnp.zeros_like(sb_ref)

    def body(j, carry):
        rf = pl.multiple_of(j * CHUNK, CHUNK)
        rb = pl.multiple_of((cps - 1 - j) * CHUNK, CHUNK)
        _ssd_chunk(xf_ref, bcf_ref, dtf_ref, par_ref, ef_ref, yf_ref, sf_ref, rf, fwd=True)
        _ssd_chunk(xb_ref, bcb_ref, dtb_ref, par_ref, eb_ref, yb_ref, sb_ref, rb, fwd=False)
        return carry

    lax.fori_loop(0, cps, body, 0, unroll=2)


def _ssd(xs, bc, dt, par, e_f, e_b, layer, cps):
    b, l, _ = xs.shape
    rows = cps * CHUNK
    ns = l // rows
    fmap = lambda bi, c: (bi, c, 0)
    bmap = lambda bi, c: (bi, ns - 1 - c, 0)
    const = lambda bi, c: (0, 0)
    state = pltpu.VMEM((N_SSD_HEADS // 2, LANE, LANE), F32)
    return pl.pallas_call(
        functools.partial(_ssd_kernel, cps=cps),
        grid=(b, ns),
        in_specs=[
            pl.BlockSpec((1, rows, D_INNER), fmap),
            pl.BlockSpec((1, rows, 2 * BC_DIM), fmap),
            pl.BlockSpec((1, rows, DT_COLS), fmap),
            pl.BlockSpec((1, rows, D_INNER), bmap),
            pl.BlockSpec((1, rows, 2 * BC_DIM), bmap),
            pl.BlockSpec((1, rows, DT_COLS), bmap),
            _layer_block((8, LANE), layer),
            pl.BlockSpec((LANE, D_INNER), const),
            pl.BlockSpec((LANE, D_INNER), const),
        ],
        out_specs=[pl.BlockSpec((1, rows, D_INNER), fmap), pl.BlockSpec((1, rows, D_INNER), bmap)],
        out_shape=[jax.ShapeDtypeStruct((b, l, D_INNER), BF16)] * 2,
        scratch_shapes=[state, state],
        compiler_params=_cparams("parallel", "arbitrary"),
        name="ssd_scan",
    )(xs, bc, dt, xs, bc, dt, par, e_f, e_b)


def _router_gates(lg_t):
    row = lambda j: lg_t[j:j + 1, :]
    lgrp = [row(j) for j in range(N_EGROUPS)]
    m = functools.reduce(jnp.maximum, lgrp)
    g_sel = jnp.where(lgrp[0] == m, 0, jnp.where(lgrp[1] == m, 1, jnp.where(lgrp[2] == m, 2, 3)))
    p_sel = 1.0 / sum(jnp.exp(v - m) for v in lgrp)
    s = []
    for k in range(EXPERTS_PER_GROUP):
        e = [row(N_EGROUPS + EXPERTS_PER_GROUP * g + k) for g in range(N_EGROUPS)]
        s.append(jnp.where(g_sel == 0, e[0], jnp.where(g_sel == 1, e[1], jnp.where(g_sel == 2, e[2], e[3]))))
    m1 = functools.reduce(jnp.maximum, s)
    i1 = jnp.where(s[0] == m1, 0, jnp.where(s[1] == m1, 1, jnp.where(s[2] == m1, 2, 3)))
    s2 = [jnp.where(i1 == k, -jnp.inf, s[k]) for k in range(EXPERTS_PER_GROUP)]
    m2 = functools.reduce(jnp.maximum, s2)
    i2 = jnp.where(s2[0] == m2, 0, jnp.where(s2[1] == m2, 1, jnp.where(s2[2] == m2, 2, 3)))
    t = jnp.exp(m2 - m1)
    w1 = 1.0 / (1.0 + t)
    w2 = t / (1.0 + t)
    first_is_a = i1 < i2
    a = jnp.minimum(i1, i2)
    b = jnp.maximum(i1, i2)
    gate_a = jnp.where(first_is_a, w1, w2) * p_sel
    gate_b = jnp.where(first_is_a, w2, w1) * p_sel
    pair = jnp.where(a == 0, 0, jnp.where(a == 1, 3, 5)) + b - a - 1
    bucket = g_sel * PAIRS_PER_GROUP + pair
    tm = lg_t.shape[1]
    rows = lax.broadcasted_iota(jnp.int32, (SUBLANES, tm), 0)
    gates = jnp.where(rows == 0, gate_a, jnp.where(rows == 1, gate_b, 0.0))
    return jnp.concatenate([gates, jnp.zeros((LANE - SUBLANES, tm), F32)], axis=0), bucket


def _pack_bf16_pairs(h):
    bits = lax.bitcast_convert_type(h.astype(BF16).astype(F32), jnp.uint32)
    half = h.shape[1] // 2
    return (bits[:, :half] >> 16) | (bits[:, half:] & jnp.uint32(0xFFFF0000))


def _unpack_bf16_pairs(w):
    lo = lax.bitcast_convert_type(w << 16, F32).astype(BF16)
    hi = lax.bitcast_convert_type(w & jnp.uint32(0xFFFF0000), F32).astype(BF16)
    return lo, hi


def _merge_kernel(yf_ref, yb_ref, xs_ref, z_ref, att_ref, gag_ref, x_ref, dsk_ref, ng_ref,
                  wso_ref, wao_ref, wo_ref, gf_ref, wrh_ref, wrhl_ref, rb_ref, upper_ref,
                  x1_ref, h2g_ref, grp_ref, rank_ref, counts_ref, cnt):
    @pl.when(pl.program_id(0) == 0)
    def _():
        cnt[...] = jnp.zeros_like(cnt)

    y = yf_ref[...].astype(F32) + yb_ref[...].astype(F32) + dsk_ref[...] * xs_ref[...].astype(F32)
    z = z_ref[...].astype(F32)
    y = y * (z * _sigmoid(z))
    gsz = D_INNER // N_SSD_GROUPS
    parts = []
    for g in range(N_SSD_GROUPS):
        yg = y[:, g * gsz:(g + 1) * gsz]
        parts.append(yg * lax.rsqrt(jnp.mean(yg * yg, axis=-1, keepdims=True) + EPS))
    ysn = (jnp.concatenate(parts, axis=1) * ng_ref[...]).astype(BF16)
    ys = _dot(ysn, wso_ref[...])
    ya = _dot(att_ref[...], wao_ref[...])
    ga = gag_ref[:, 0:D_MODEL].astype(F32)
    gs = gag_ref[:, D_MODEL:2 * D_MODEL].astype(F32)
    mixed = _sigmoid(ga) * ya + _sigmoid(gs) * ys
    x1 = x_ref[...] + _dot(mixed.astype(BF16), wo_ref[...])
    x1_ref[...] = x1
    h2 = _rmsnorm(x1, gf_ref[...])
    h2g_ref[:, 0:ROUTED_WORDS] = _pack_bf16_pairs(h2)
    hh, hl = _split_bf16(h2, 2)
    hi_terms = _dot(hh, wrhl_ref[...])
    lg = hi_terms[:, 0:LANE] + _dot(hl, wrh_ref[...]) + hi_terms[:, LANE:] + rb_ref[...]
    tm = x1.shape[0]
    gates_t, bucket = _router_gates(lg.T)
    h2g_ref[:, ROUTED_WORDS:] = lax.bitcast_convert_type(gates_t.T, jnp.uint32)
    onehot = (lax.broadcasted_iota(jnp.int32, (BUCKET_ROWS, tm), 0) == bucket).astype(F32)
    rank_all = _dot(onehot.astype(BF16), upper_ref[...]) + cnt[:, 0:1]
    rank_ref[0] = jnp.sum(onehot * rank_all, axis=0, keepdims=True).astype(jnp.int32)
    grp_ref[0] = bucket
    cnt[...] += jnp.sum(onehot, axis=1, keepdims=True)
    counts_ref[...] = cnt[...]


def _merge(yf, yb, xs, z, att, gag, x, dsk, ng, wso, wao, wo, gf, wrh, wrhl, rb, layer, tm):
    t = x.shape[0]
    nt = t // tm
    upper = jnp.asarray(np.triu(np.ones((tm, tm), np.float32), 1), BF16)
    tile = lambda n: pl.BlockSpec((tm, n), lambda i: (i, 0))
    per_layer = lambda a: _layer_block(a.shape[1:], layer)
    row = pl.BlockSpec((1, 1, tm), lambda i: (i, 0, 0))
    consts = (dsk, ng, wso, wao, wo, gf, wrh, wrhl, rb)
    return pl.pallas_call(
        _merge_kernel,
        grid=(nt,),
        in_specs=[tile(D_INNER), tile(D_INNER), tile(D_INNER), tile(D_INNER), tile(Q_DIM),
                  tile(GATE_COLS), tile(D_MODEL)] + [per_layer(a) for a in consts]
                 + [pl.BlockSpec((tm, tm), lambda i: (0, 0))],
        out_specs=[tile(D_MODEL), tile(ROUTED_WORDS + LANE), row, row, pl.BlockSpec((BUCKET_ROWS, LANE), lambda i: (0, 0))],
        out_shape=[jax.ShapeDtypeStruct((t, D_MODEL), F32),
                   jax.ShapeDtypeStruct((t, ROUTED_WORDS + LANE), jnp.uint32),
                   jax.ShapeDtypeStruct((nt, 1, tm), jnp.int32),
                   jax.ShapeDtypeStruct((nt, 1, tm), jnp.int32),
                   jax.ShapeDtypeStruct((BUCKET_ROWS, LANE), F32)],
        scratch_shapes=[pltpu.VMEM((BUCKET_ROWS, LANE), F32)],
        compiler_params=_cparams("arbitrary"),
        name="merge_router",
    )(yf, yb, xs, z, att, gag, x, *consts, upper)


def _dma_params(*sem):
    return pltpu.CompilerParams(dimension_semantics=sem, vmem_limit_bytes=VMEM_LIMIT_BYTES,
                                disable_bounds_checks=True)


def _permute_kernel(pos_ref, src_ref, init_ref, dst_ref, stage, sem, *, rows, nt):
    del init_ref
    i = pl.program_id(0)
    slot = i % 2
    stage[slot] = src_ref[...]

    def issue(r8, carry):
        base = pl.multiple_of(r8 * SUBLANES, SUBLANES)
        src_tile = stage.at[slot, pl.ds(base, SUBLANES)]
        for k in range(SUBLANES):
            pltpu.make_async_copy(src_tile.at[pl.ds(k, 1)], dst_ref.at[pl.ds(pos_ref[0, 0, base + k], 1)],
                                  sem.at[slot]).start(priority=k % 2)
        return carry

    lax.fori_loop(0, rows // SUBLANES, issue, 0)

    def wait_all(s):
        pltpu.make_async_copy(stage.at[s], dst_ref.at[pl.ds(0, rows)], sem.at[s]).wait()

    @pl.when(i > 0)
    def _():
        wait_all(1 - slot)

    @pl.when(i == nt - 1)
    def _():
        wait_all(slot)


def _permute(h2g, pos, t_pad, rows):
    t, w = h2g.shape
    nt = t // rows
    return pl.pallas_call(
        functools.partial(_permute_kernel, rows=rows, nt=nt),
        grid=(nt,),
        in_specs=[pl.BlockSpec((1, 1, rows), lambda i: (i, 0, 0), memory_space=pltpu.SMEM),
                  pl.BlockSpec((rows, w), lambda i: (i, 0)),
                  pl.BlockSpec(memory_space=pl.ANY)],
        out_specs=pl.BlockSpec(memory_space=pl.ANY),
        out_shape=jax.ShapeDtypeStruct((t_pad, w), h2g.dtype),
        scratch_shapes=[pltpu.VMEM((2, rows, w), h2g.dtype), pltpu.SemaphoreType.DMA((2,))],
        input_output_aliases={2: 0},
        compiler_params=_dma_params("arbitrary"),
        name="moe_permute",
    )(pos.reshape(nt, 1, rows), h2g, jnp.zeros((t_pad, w), h2g.dtype))


def _experts_kernel(ea_ref, eb_ref, tv_ref, hs_ref, wgu_a, wd_a, wgu_b, wd_b, o_ref):
    del ea_ref, eb_ref
    i = pl.program_id(0)

    @pl.when(tv_ref[i] == 0)
    def _():
        o_ref[...] = jnp.zeros_like(o_ref)

    @pl.when(tv_ref[i] != 0)
    def _():
        lo, hi = _unpack_bf16_pairs(hs_ref[:, 0:ROUTED_WORDS])
        gates = lax.bitcast_convert_type(hs_ref[:, ROUTED_WORDS:], F32)
        acc = jnp.zeros(o_ref.shape, F32)
        for lane, (wgu, wd) in enumerate(((wgu_a, wd_a), (wgu_b, wd_b))):
            gu = _dot(lo, wgu[0, 0:ROUTED_WORDS, :]) + _dot(hi, wgu[0, ROUTED_WORDS:, :])
            g = gu[:, 0:D_EXPERT]
            u = gu[:, D_EXPERT:2 * D_EXPERT]
            a = (g * _sigmoid(g)) * u * gates[:, lane:lane + 1]
            acc = acc + _dot(a.astype(BF16), wd[0])
        o_ref[...] = acc


def _experts(hs, expert_a, expert_b, tile_valid, wgu, wd, layer, tms):
    t_pad, w = hs.shape
    pick_a = lambda i, ea, eb, tv: (layer, ea[i], 0, 0)
    pick_b = lambda i, ea, eb, tv: (layer, eb[i], 0, 0)
    return pl.pallas_call(
        _experts_kernel,
        grid_spec=pltpu.PrefetchScalarGridSpec(
            num_scalar_prefetch=3,
            grid=(t_pad // tms,),
            in_specs=[
                pl.BlockSpec((tms, w), lambda i, ea, eb, tv: (i, 0)),
                pl.BlockSpec((None, 1, D_MODEL, 2 * D_EXPERT), pick_a),
                pl.BlockSpec((None, 1, D_EXPERT, D_MODEL), pick_a),
                pl.BlockSpec((None, 1, D_MODEL, 2 * D_EXPERT), pick_b),
                pl.BlockSpec((None, 1, D_EXPERT, D_MODEL), pick_b),
            ],
            out_specs=pl.BlockSpec((tms, D_MODEL), lambda i, ea, eb, tv: (i, 0)),
        ),
        out_shape=jax.ShapeDtypeStruct((t_pad, D_MODEL), F32),
        compiler_params=_cparams("arbitrary"),
        name="moe_experts",
    )(expert_a, expert_b, tile_valid, hs, wgu, wd, wgu, wd)


def _ple_kernel(pos_ref, posn_ref, x_ref, p_ref, g_ref, wg_ref, wp_ref, gfin_ref, ys_ref, o_ref, buf, sem,
                *, tm, nt, final):
    i = pl.program_id(0)
    slot = i % 2

    def gather(pref, s):
        def issue(r8, carry):
            base = pl.multiple_of(r8 * SUBLANES, SUBLANES)
            dst_tile = buf.at[s, pl.ds(base, SUBLANES)]
            for k in range(SUBLANES):
                pltpu.make_async_copy(ys_ref.at[pl.ds(pref[0, 0, base + k], 1)], dst_tile.at[pl.ds(k, 1)],
                                      sem.at[s]).start(priority=k % 2)
            return carry
        lax.fori_loop(0, tm // SUBLANES, issue, 0)

    @pl.when(i == 0)
    def _():
        gather(pos_ref, 0)

    @pl.when(i + 1 < nt)
    def _():
        gather(posn_ref, 1 - slot)

    pltpu.make_async_copy(ys_ref.at[pl.ds(0, tm)], buf.at[slot], sem.at[slot]).wait()
    x = x_ref[...] + buf[slot]
    gate = _sigmoid(_dot(_rmsnorm(x, g_ref[...]).astype(BF16), wg_ref[...]))
    y = x + gate * _dot(p_ref[...].astype(BF16), wp_ref[...])
    o_ref[...] = _rmsnorm(y, gfin_ref[...]) if final else y


def _ple(x, ys, pos, p, g, wg, wp, g_final, layer, final, tm):
    t = x.shape[0]
    nt = t // tm
    pos3 = pos.reshape(nt, 1, tm)
    return pl.pallas_call(
        functools.partial(_ple_kernel, tm=tm, nt=nt, final=final),
        grid=(nt,),
        in_specs=[
            pl.BlockSpec((1, 1, tm), lambda i: (i, 0, 0), memory_space=pltpu.SMEM),
            pl.BlockSpec((1, 1, tm), lambda i: (jnp.minimum(i + 1, nt - 1), 0, 0), memory_space=pltpu.SMEM),
            pl.BlockSpec((tm, D_MODEL), lambda i: (i, 0)),
            pl.BlockSpec((None, tm, PLE_DIM), lambda i: (layer, i, 0)),
            _layer_block((1, D_MODEL), layer),
            _layer_block((D_MODEL, D_MODEL), layer),
            _layer_block((PLE_DIM, D_MODEL), layer),
            pl.BlockSpec((1, D_MODEL), lambda i: (0, 0)),
            pl.BlockSpec(memory_space=pl.ANY),
        ],
        out_specs=pl.BlockSpec((tm, D_MODEL), lambda i: (i, 0)),
        out_shape=jax.ShapeDtypeStruct((t, D_MODEL), F32),
        scratch_shapes=[pltpu.VMEM((2, tm, D_MODEL), F32), pltpu.SemaphoreType.DMA((2,))],
        compiler_params=_dma_params("arbitrary"),
        name="ple_gate",
    )(pos3, pos3, x, p, g, wg, wp, g_final, ys)


def _route(bucket, rank, counts, t, tms):
    cnt = counts[:N_BUCKETS_MOE, 0].astype(jnp.int32)
    padded = (cnt + tms - 1) // tms * tms
    ends = jnp.cumsum(padded)
    starts = ends - padded
    bucket = bucket.reshape(t)
    pos = rank.reshape(t) + sum(jnp.where(bucket == k, starts[k], 0) for k in range(N_BUCKETS_MOE))
    t_pad = t + N_BUCKETS_MOE * tms
    tile_start = jnp.arange(t_pad // tms, dtype=jnp.int32) * tms
    tile_bucket = jnp.minimum(sum((tile_start >= ends[k]).astype(jnp.int32) for k in range(N_BUCKETS_MOE)),
                              N_BUCKETS_MOE - 1)
    tile_valid = (tile_start < ends[N_BUCKETS_MOE - 1]).astype(jnp.int32)
    pairs = [(a, b) for a in range(EXPERTS_PER_GROUP) for b in range(a + 1, EXPERTS_PER_GROUP)]
    group, pair = tile_bucket // PAIRS_PER_GROUP, tile_bucket % PAIRS_PER_GROUP
    first = sum(jnp.where(pair == k, a, 0) for k, (a, _) in enumerate(pairs))
    second = sum(jnp.where(pair == k, b, 0) for k, (_, b) in enumerate(pairs))
    expert_a = (group * EXPERTS_PER_GROUP + first).astype(jnp.int32)
    expert_b = (group * EXPERTS_PER_GROUP + second).astype(jnp.int32)
    return pos, expert_a, expert_b, tile_valid, t_pad


def _band_bias(rel_bias):
    rel = (np.arange(3 * BLK)[None, :] - BLK) - np.arange(BLK)[:, None]
    nb = N_BUCKETS // 2
    max_exact = nb // 2
    ret = (rel > 0).astype(np.int32) * nb
    n = np.abs(rel)
    large = max_exact + (np.log(np.maximum(n, 1) / max_exact) / np.log(MAX_DISTANCE / max_exact)
                         * (nb - max_exact)).astype(np.int32)
    large = np.minimum(large, nb - 1)
    bucket = (ret + np.where(n < max_exact, n, large)).astype(np.int32)
    table = rel_bias.astype(F32)
    bucket = jnp.asarray(bucket)[None]
    bias = jnp.zeros((N_HEADS, BLK, 3 * BLK), F32)
    for k in range(N_BUCKETS):
        bias = jnp.where(bucket == k, table[k][:, None, None], bias)
    bias = jnp.where(jnp.asarray(n <= WINDOW)[None], bias, MASKED)
    bias = bias[np.asarray(STACK_HEAD_ORDER)].reshape(N_HEADS * BLK, 3 * BLK)
    key_blk = jnp.arange(3 * BLK) // BLK
    no_left = jnp.where(key_blk == 0, MASKED, bias)
    no_right = jnp.where(key_blk == 2, MASKED, bias)
    return jnp.stack([bias, no_left, no_right, jnp.where(key_blk == 2, MASKED, no_left)])


def _by_head(a, axis):
    return jnp.concatenate([lax.slice_in_dim(a, h * HEAD_DIM, (h + 1) * HEAD_DIM, axis=axis)
                            for h in STACK_HEAD_ORDER], axis=axis)


def _head_expand_matrix(first_lane):
    e = np.zeros((LANE, D_INNER), np.float32)
    for h in range(N_SSD_HEADS):
        e[first_lane + h, h * SSD_HEAD_DIM:(h + 1) * SSD_HEAD_DIM] = 1.0
    return jnp.asarray(e, BF16)


def _prep_layers(P):
    depth = P["w_in"].shape[0]
    w = P["w_in"]
    o = np.cumsum([0, Q_DIM, KV_DIM, KV_DIM, D_INNER, CONV_DIM, N_SSD_HEADS, N_SSD_HEADS, D_MODEL, D_MODEL])
    q, k, v, z, xbc, dtf, dtb, ga, gs = [w[:, :, o[i]:o[i + 1]] for i in range(9)]
    dt_pad = jnp.zeros((depth, D_MODEL, DT_COLS - 2 * N_SSD_HEADS), F32)
    w_in = jnp.concatenate([_by_head(q, 2) * (HEAD_DIM ** -0.5), k, v, z, xbc, ga, gs, dtf, dtb, dt_pad],
                           axis=-1).astype(BF16)

    pad_lanes = lambda a: jnp.pad(a, ((0, 0), (0, LANE - a.shape[-1])))
    par = jnp.stack([pad_lanes(jnp.concatenate([P["dt_bias_f"], P["dt_bias_b"]], axis=-1)),
                     pad_lanes(jnp.concatenate([P["a_log_f"], P["a_log_b"]], axis=-1))], axis=1)
    par = jnp.pad(par.astype(F32), ((0, 0), (0, 6), (0, 0)))

    wr = jnp.concatenate([P["router_g"], P["router_e"]], axis=-1)
    wr = jnp.pad(wr, ((0, 0), (0, 0), (0, LANE - wr.shape[-1])))
    wr_hi = wr.astype(BF16)
    wr_hilo = jnp.concatenate([wr_hi, (wr - wr_hi.astype(F32)).astype(BF16)], axis=-1)
    rb = pad_lanes(jnp.concatenate([P["router_g_b"], P["router_e_b"]], axis=-1))[:, None, :]

    row = lambda a: a.astype(F32)[:, None, :]
    return dict(
        g_mix=row(P["g_mix"]), w_in=w_in, sink=jnp.broadcast_to(jnp.repeat(P["attn_sink"].astype(F32)[:, np.asarray(STACK_HEAD_ORDER)], BLK, axis=1)[:, :, None],
                              (depth, N_HEADS * BLK, LANE)),
        w_attn_out=_by_head(P["w_attn_out"], 1).astype(BF16),
        conv_w=P["conv_w"].astype(F32), conv_b=row(P["conv_b"]), par=par,
        d_skip=row(jnp.repeat(P["d_skip"], SSD_HEAD_DIM, axis=-1)), ssd_norm_g=row(P["ssd_norm_g"]),
        w_ssd_out=P["w_ssd_out"].astype(BF16), w_out=P["w_out"].astype(BF16),
        g_ffn=row(P["g_ffn"]), wr_hi=wr_hi, wr_hilo=wr_hilo, rb=rb.astype(F32),
        w_gate_up=P["w_gate_up"].astype(BF16), w_down=P["w_down"].astype(BF16),
        g_ple=row(P["g_ple"]), w_ple_gate=P["w_ple_gate"].astype(BF16),
        w_ple_proj=P["w_ple_proj"].astype(BF16),
    )


def _tile(n, want):
    t = min(n, want)
    assert n % t == 0
    return t


def _trunk(x, p, layers, bias, e_f, e_b, g_final):
    b, l, _ = x.shape
    t = b * l
    tm = _tile(t, 512)
    seq_tile = _tile(l, 512)
    cps = seq_tile // CHUNK

    W = layers
    depth = p.shape[0]
    p = p.reshape(depth, t, PLE_DIM)
    g_final = g_final.astype(F32)[None, :]
    xf = x.reshape(t, D_MODEL)
    for i in range(depth):
        qkv, z, gag, dt, xs, bc = _inproj(xf, W["g_mix"], W["w_in"], W["conv_w"], W["conv_b"], i, tm, l)
        att = _attention(qkv.reshape(b, l, QKV_COLS), W["sink"], bias, i, seq_tile)
        yf, yb = _ssd(xs.reshape(b, l, D_INNER), bc.reshape(b, l, 2 * BC_DIM), dt.reshape(b, l, DT_COLS),
                      W["par"], e_f, e_b, i, cps)
        x1, h2g, grp, rank, counts = _merge(
            yf.reshape(t, D_INNER), yb.reshape(t, D_INNER), xs.reshape(t, D_INNER), z, att.reshape(t, Q_DIM), gag,
            xf, W["d_skip"], W["ssd_norm_g"], W["w_ssd_out"], W["w_attn_out"], W["w_out"], W["g_ffn"],
            W["wr_hi"], W["wr_hilo"], W["rb"], i, tm)
        pos, expert_a, expert_b, tile_valid, t_pad = _route(grp, rank, counts, t, tm)
        hs = _permute(h2g, pos, t_pad, tm)
        ys = _experts(hs, expert_a, expert_b, tile_valid, W["w_gate_up"], W["w_down"], i, tm)
        xf = _ple(x1, ys, pos, p, W["g_ple"], W["w_ple_gate"], W["w_ple_proj"], g_final, i, i == depth - 1, tm)
    return xf.reshape(b, l, D_MODEL)


def kernel(x_prompt, x_sample, p_prompt, p_sample, g_mix, w_in, rel_bias, attn_sink, w_attn_out, conv_w, conv_b, a_log_f, a_log_b, dt_bias_f, dt_bias_b, d_skip, ssd_norm_g, w_ssd_out, w_out, g_ffn, router_g, router_g_b, router_e, router_e_b, w_gate_up, w_down, g_ple, w_ple_gate, w_ple_proj, g_final):
    P = dict(g_mix=g_mix, w_in=w_in, attn_sink=attn_sink, w_attn_out=w_attn_out, conv_w=conv_w, conv_b=conv_b,
             a_log_f=a_log_f, a_log_b=a_log_b, dt_bias_f=dt_bias_f, dt_bias_b=dt_bias_b, d_skip=d_skip,
             ssd_norm_g=ssd_norm_g, w_ssd_out=w_ssd_out, w_out=w_out, g_ffn=g_ffn, router_g=router_g,
             router_g_b=router_g_b, router_e=router_e, router_e_b=router_e_b, w_gate_up=w_gate_up,
             w_down=w_down, g_ple=g_ple, w_ple_gate=w_ple_gate, w_ple_proj=w_ple_proj)
    layers = _prep_layers(P)
    bias = _band_bias(rel_bias)
    e_f = _head_expand_matrix(0)
    e_b = _head_expand_matrix(N_SSD_HEADS)
    y_prompt = _trunk(x_prompt, p_prompt, layers, bias, e_f, e_b, g_final)
    y_sample = _trunk(x_sample, p_sample, layers, bias, e_f, e_b, g_final)
    return (y_prompt, y_sample)
```
